```python
import math
import jax, jax.numpy as jnp
from jax import lax
import numpy as np

D_MODEL = 2048
BATCH = 8
SEQ = 4096
DEPTH = 4

CTX_LEN = 256
GRID_W = 64
Q_BLOCK = 128
ROPE_DIM = 64
ROPE_BASE = 10000.0

MLA_HEADS = 4
MLA_NOPE = 128
MLA_ROPE = ROPE_DIM
MLA_V = 128
MLA_Q_RANK = 384
MLA_KV_RANK = 256
DIFF_HEADS = 4
DIFF_QK = 64
DIFF_V = 2 * DIFF_QK
DIFF_QK_W = DIFF_HEADS * 2 * DIFF_QK
CHUNK = 128
GMLP_GROUPS = 4
GMLP_GROUP_W = 128
GMLP_W = GMLP_GROUPS * GMLP_GROUP_W
POOL_WINDOWS = (2, 4, 8, 16)
POOL_GROUP_W = 128
POOL_W = len(POOL_WINDOWS) * POOL_GROUP_W
N_BRANCH = 4
BRANCH_W = 512
KV_SPLITS = (MLA_KV_RANK, MLA_KV_RANK + MLA_ROPE, MLA_KV_RANK + MLA_ROPE + DIFF_QK_W)
KV_COLS = MLA_KV_RANK + MLA_ROPE + DIFF_QK_W + DIFF_HEADS * DIFF_V
REST_SPLITS = (MLA_Q_RANK, MLA_Q_RANK + DIFF_QK_W, MLA_Q_RANK + DIFF_QK_W + 2 * GMLP_W,
               MLA_Q_RANK + DIFF_QK_W + 2 * GMLP_W + POOL_W)
REST_COLS = MLA_Q_RANK + DIFF_QK_W + 2 * GMLP_W + POOL_W + N_BRANCH * D_MODEL
N_IN = KV_COLS + REST_COLS
N_EXPERTS = 32
N_GROUPS = 4
EXPERTS_PER_GROUP = N_EXPERTS // N_GROUPS
TOP_K = 2
D_EXPERT = 512
DEEPNORM_ALPHA = (2 * DEPTH) ** 0.25
DEEPNORM_BETA = (8 * DEPTH) ** -0.25
LN_EPS = 1e-5
RMS_EPS = 1e-6

kernel_name = 'hybrid_diffusion_mla_diff_gmlp_pool_moe'


def _layernorm(x, g, b):
    xf = x.astype(jnp.float32)
    mu = jnp.mean(xf, axis=-1, keepdims=True)
    var = jnp.mean(jnp.square(xf - mu), axis=-1, keepdims=True)
    y = (xf - mu) * lax.rsqrt(var + LN_EPS) * g.astype(jnp.float32) + b.astype(jnp.float32)
    return y.astype(x.dtype)


def _rmsnorm(x, g):
    xf = x.astype(jnp.float32)
    y = xf * lax.rsqrt(jnp.mean(jnp.square(xf), axis=-1, keepdims=True) + RMS_EPS)
    return (y * g.astype(jnp.float32)).astype(x.dtype)


def _axial_rope(rows):
    row = jnp.repeat(jnp.arange(rows, dtype=jnp.float32), GRID_W)
    col = jnp.tile(jnp.arange(GRID_W, dtype=jnp.float32), rows)
    n_freq = ROPE_DIM // 4
    inv = ROPE_BASE ** (-jnp.arange(n_freq, dtype=jnp.float32) / n_freq)
    ang = jnp.concatenate([row[:, None] * inv, col[:, None] * inv], axis=-1)
    return jnp.cos(ang), jnp.sin(ang)


def _rope(x, cos, sin):
    half = x.shape[-1] // 2
    x1, x2 = x[..., :half], x[..., half:]
    cos = cos.astype(x.dtype)
    sin = sin.astype(x.dtype)
    return jnp.concatenate([x1 * cos - x2 * sin, x1 * sin + x2 * cos], axis=-1)


def _sweep_query_blocks(block_fn, q):
    lead = q.shape[:-2]
    lq, d = q.shape[-2], q.shape[-1]
    nb = lq // Q_BLOCK
    qb = jnp.moveaxis(q.reshape(lead + (nb, Q_BLOCK, d)), -3, 0)
    out = jnp.moveaxis(lax.map(block_fn, qb), 0, -3)
    return out.reshape(out.shape[:-3] + (lq, out.shape[-1]))


def _mla_attention(q, k, v):
    scale = (MLA_NOPE + MLA_ROPE) ** -0.5

    def block(qb):
        s = jnp.einsum('bhqd,bhkd->bhqk', qb, k, preferred_element_type=jnp.float32) * scale
        p = jax.nn.softmax(s, axis=-1)
        return jnp.einsum('bhqk,bhkd->bhqd', p.astype(v.dtype), v)

    return _sweep_query_blocks(block, q)


def _diff_attention(q, k, v, lam):
    scale = DIFF_QK ** -0.5

    def block(qb):
        s = jnp.einsum('bhjqd,bhjkd->bhjqk', qb, k, preferred_element_type=jnp.float32) * scale
        p = jax.nn.softmax(s, axis=-1)
        w = p[:, :, 0] - lam * p[:, :, 1]
        return jnp.einsum('bhqk,bhkd->bhqd', w.astype(v.dtype), v)

    return _sweep_query_blocks(block, q)


def _kv_side(proj_kv, lp, rope):
    b, l, _ = proj_kv.shape
    c_kv, k_pe, dk, dv = jnp.split(proj_kv, KV_SPLITS, axis=-1)
    c_kv = _rmsnorm(c_kv, lp['mla_kv_norm'])
    kv = (c_kv @ lp['mla_w_ukv']).reshape(b, l, MLA_HEADS, MLA_NOPE + MLA_V).transpose(0, 2, 1, 3)
    k_nope, mla_v = kv[..., :MLA_NOPE], kv[..., MLA_NOPE:]
    k_pe = k_pe[:, None]
    if rope is not None:
        k_pe = _rope(k_pe, rope[0], rope[1])
    mla_k = jnp.concatenate([k_nope, jnp.broadcast_to(k_pe, (b, MLA_HEADS, l, MLA_ROPE))], axis=-1)
    diff_k = dk.reshape(b, l, DIFF_HEADS, 2, DIFF_QK).transpose(0, 2, 3, 1, 4)
    if rope is not None:
        diff_k = _rope(diff_k, rope[0], rope[1])
    diff_v = dv.reshape(b, l, DIFF_HEADS, DIFF_V).transpose(0, 2, 1, 3)
    return (mla_k, mla_v, diff_k, diff_v)


def _chunk_gmlp(z, ln_g, ln_b, ws, bs):
    b, l, _ = z.shape
    z = jax.nn.gelu(z)
    u, v = jnp.split(z, 2, axis=-1)
    v = _layernorm(v, ln_g, ln_b)
    v = v.reshape(b, l // CHUNK, CHUNK, GMLP_GROUPS, GMLP_GROUP_W)
    s = jnp.einsum('gpq,bnqgc->bnpgc', ws, v) + bs.T[:, :, None]
    return u * s.reshape(b, l, GMLP_W)


def _multiscale_pool(p, pool_w, pool_scale):
    b, l, _ = p.shape
    t = jnp.arange(l)
    cs = jnp.pad(jnp.cumsum(p.astype(jnp.float32), axis=1), ((0, 0), (1, 0), (0, 0)))
    outs = []
    for i, w in enumerate(POOL_WINDOWS):
        lo = jnp.clip(t - w // 2, 0, l)
        hi = jnp.clip(t - w // 2 + w, 0, l)
        ch = slice(i * POOL_GROUP_W, (i + 1) * POOL_GROUP_W)
        seg = cs[:, :, ch]
        mean = (jnp.take(seg, hi, axis=1) - jnp.take(seg, lo, axis=1)) / (hi - lo).astype(jnp.float32)[None, :, None]
        resid = (mean - p[:, :, ch].astype(jnp.float32)).astype(p.dtype)
        outs.append(resid @ pool_w[i])
    return jnp.concatenate(outs, axis=-1) * pool_scale


def _token_mixer(proj_rest, kv_self, kv_ctx, rope, lp, lam, lam_init):
    b, l, _ = proj_rest.shape
    c_q, dq, gm, pl, gate_logits = jnp.split(proj_rest, REST_SPLITS, axis=-1)
    mla_k, mla_v, diff_k, diff_v = kv_self
    if kv_ctx is not None:
        mla_k = jnp.concatenate([kv_ctx[0], mla_k], axis=-2)
        mla_v = jnp.concatenate([kv_ctx[1], mla_v], axis=-2)
        diff_k = jnp.concatenate([kv_ctx[2], diff_k], axis=-2)
        diff_v = jnp.concatenate([kv_ctx[3], diff_v], axis=-2)
    c_q = _rmsnorm(c_q, lp['mla_q_norm'])
    q = (c_q @ lp['mla_w_uq']).reshape(b, l, MLA_HEADS, MLA_NOPE + MLA_ROPE).transpose(0, 2, 1, 3)
    q_nope, q_pe = q[..., :MLA_NOPE], q[..., MLA_NOPE:]
    if rope is not None:
        q_pe = _rope(q_pe, rope[0], rope[1])
    q = jnp.concatenate([q_nope, q_pe], axis=-1)
    mla_out = _mla_attention(q, mla_k, mla_v).transpose(0, 2, 1, 3).reshape(b, l, MLA_HEADS * MLA_V)
    dq = dq.reshape(b, l, DIFF_HEADS, 2, DIFF_QK).transpose(0, 2, 3, 1, 4)
    if rope is not None:
        dq = _rope(dq, rope[0], rope[1])
    d_out = _diff_attention(dq, diff_k, diff_v, lam)
    d_out = _rmsnorm(d_out, lp['diff_subln']) * (1.0 - lam_init)
    d_out = d_out.transpose(0, 2, 1, 3).reshape(b, l, DIFF_HEADS * DIFF_V)
    g_out = _chunk_gmlp(gm, lp['gmlp_ln_g'], lp['gmlp_ln_b'], lp['gmlp_ws'], lp['gmlp_bs'])
    p_out = _multiscale_pool(pl, lp['pool_w'], lp['pool_scale'])
    gates = jax.nn.sigmoid(gate_logits.astype(jnp.float32)).astype(proj_rest.dtype)
    gates = gates.reshape(b, l, N_BRANCH, D_MODEL)
    merged = None
    for n, br in enumerate((mla_out, d_out, g_out, p_out)):
        term = gates[:, :, n] * (br @ lp['w_branch'][n])
        merged = term if merged is None else merged + term
    return merged @ lp['w_out']


def _moe(h, router_w, router_bias, w1, w3, w2):
    logits = jnp.einsum('bld,de->ble', h, router_w, preferred_element_type=jnp.float32)
    scores = jax.nn.sigmoid(logits)
    biased = (scores + router_bias.astype(jnp.float32)).reshape(scores.shape[:-1] + (N_GROUPS, EXPERTS_PER_GROUP))
    group_score = jnp.sum(lax.top_k(biased, TOP_K)[0], axis=-1)
    best = jnp.argmax(group_score, axis=-1)
    in_group = (jnp.arange(N_GROUPS) == best[..., None])[..., None]
    masked = jnp.where(in_group, biased, -jnp.inf).reshape(scores.shape)
    _, idx = lax.top_k(masked, TOP_K)
    w = jnp.take_along_axis(scores, idx, axis=-1)
    w = w / jnp.sum(w, axis=-1, keepdims=True)
    gate = jnp.sum(jax.nn.one_hot(idx, N_EXPERTS, dtype=jnp.float32) * w[..., None], axis=-2).astype(h.dtype)
    out = jnp.zeros_like(h)
    for e in range(N_EXPERTS):
        hid = jax.nn.silu(h @ w1[e]) * (h @ w3[e])
        out = out + gate[..., e:e + 1] * (hid @ w2[e])
    return out


def setup_inputs(seed: int = 0) -> dict:
    key = jax.random.key(seed)
    ks = jax.random.split(key, 32)
    D = D_MODEL

    def nrm(k, shape, scale=1.0):
        return jax.random.normal(k, shape, jnp.float32) * scale

    return {
        'x': nrm(ks[0], (BATCH, SEQ, D)),
        'c': nrm(ks[1], (BATCH, D)),
        'ctx': nrm(ks[2], (BATCH, CTX_LEN, D)),
        'c_ctx': nrm(ks[3], (D,)),
        'ada_w': nrm(ks[4], (DEPTH, D, 6 * D), 0.5 * D ** -0.5),
        'ada_b': nrm(ks[5], (DEPTH, 6 * D), 0.02),
        'w_in': nrm(ks[6], (DEPTH, D, N_IN), D ** -0.5),
        'mla_q_norm': 1.0 + nrm(ks[7], (DEPTH, MLA_Q_RANK), 0.02),
        'mla_kv_norm': 1.0 + nrm(ks[8], (DEPTH, MLA_KV_RANK), 0.02),
        'mla_w_uq': nrm(ks[9], (DEPTH, MLA_Q_RANK, MLA_HEADS * (MLA_NOPE + MLA_ROPE)), MLA_Q_RANK ** -0.5),
        'mla_w_ukv': nrm(ks[10], (DEPTH, MLA_KV_RANK, MLA_HEADS * (MLA_NOPE + MLA_V)), MLA_KV_RANK ** -0.5),
        'diff_lambda': nrm(ks[11], (DEPTH, 4, DIFF_QK), 0.1),
        'diff_subln': 1.0 + nrm(ks[12], (DEPTH, DIFF_V), 0.02),
        'gmlp_ln_g': 1.0 + nrm(ks[13], (DEPTH, GMLP_W), 0.02),
        'gmlp_ln_b': nrm(ks[14], (DEPTH, GMLP_W), 0.02),
        'gmlp_ws': nrm(ks[15], (DEPTH, GMLP_GROUPS, CHUNK, CHUNK), CHUNK ** -0.5),
        'gmlp_bs': 1.0 + nrm(ks[16], (DEPTH, GMLP_GROUPS, CHUNK), 0.02),
        'pool_w': nrm(ks[17], (DEPTH, len(POOL_WINDOWS), POOL_GROUP_W, POOL_GROUP_W), POOL_GROUP_W ** -0.5),
        'pool_scale': 1.0 + nrm(ks[18], (DEPTH, POOL_W), 0.1),
        'w_branch': nrm(ks[19], (DEPTH, N_BRANCH, BRANCH_W, D), BRANCH_W ** -0.5),
        'w_out': nrm(ks[20], (DEPTH, D, D), DEEPNORM_BETA * D ** -0.5),
        'ln1_g': 1.0 + nrm(ks[21], (DEPTH, D), 0.02),
        'ln1_b': nrm(ks[22], (DEPTH, D), 0.02),
        'ln2_g': 1.0 + nrm(ks[23], (DEPTH, D), 0.02),
        'ln2_b': nrm(ks[24], (DEPTH, D), 0.02),
        'router_w': nrm(ks[25], (D, N_EXPERTS), D ** -0.5),
        'router_bias': nrm(ks[26], (N_EXPERTS,), 0.01),
        'moe_w1': nrm(ks[27], (DEPTH, N_EXPERTS, D, D_EXPERT), D ** -0.5),
        'moe_w3': nrm(ks[28], (DEPTH, N_EXPERTS, D, D_EXPERT), D ** -0.5),
        'moe_w2': nrm(ks[29], (DEPTH, N_EXPERTS, D_EXPERT, D), DEEPNORM_BETA * D_EXPERT ** -0.5),
    }


def reference(x, c, ctx, c_ctx, ada_w, ada_b, w_in, mla_q_norm, mla_kv_norm, mla_w_uq, mla_w_ukv,
              diff_lambda, diff_subln, gmlp_ln_g, gmlp_ln_b, gmlp_ws, gmlp_bs, pool_w, pool_scale,
              w_branch, w_out, ln1_g, ln1_b, ln2_g, ln2_b, router_w, router_bias, moe_w1, moe_w3, moe_w2):
    n_lat = x.shape[1]
    n_ctx = ctx.shape[1]
    rows = n_lat // GRID_W
    rope = _axial_rope(rows)
    s_lat = jax.nn.silu(c)
    s_ctx = jax.nn.silu(c_ctx)
    for l in range(DEPTH):
        last = l == DEPTH - 1
        lp = {
            'mla_q_norm': mla_q_norm[l], 'mla_kv_norm': mla_kv_norm[l],
            'mla_w_uq': mla_w_uq[l], 'mla_w_ukv': mla_w_ukv[l],
            'diff_subln': diff_subln[l],
            'gmlp_ln_g': gmlp_ln_g[l], 'gmlp_ln_b': gmlp_ln_b[l],
            'gmlp_ws': gmlp_ws[l], 'gmlp_bs': gmlp_bs[l],
            'pool_w': pool_w[l], 'pool_scale': pool_scale[l],
            'w_branch': w_branch[l], 'w_out': w_out[l],
        }
        lam_init = 0.8 - 0.6 * math.exp(-0.3 * l)
        lam_vec = diff_lambda[l].astype(jnp.float32)
        lam = jnp.exp(jnp.sum(lam_vec[0] * lam_vec[1])) - jnp.exp(jnp.sum(lam_vec[2] * lam_vec[3])) + lam_init
        m_lat = (s_lat @ ada_w[l] + ada_b[l])[:, None, :]
        m_ctx = (s_ctx @ ada_w[l] + ada_b[l])[None, None, :]
        sh1, sc1, g1, sh2, sc2, g2 = jnp.split(m_lat, 6, axis=-1)
        csh1, csc1, cg1, csh2, csc2, cg2 = jnp.split(m_ctx, 6, axis=-1)
        w_kv = w_in[l][:, :KV_COLS]
        w_rest = w_in[l][:, KV_COLS:]
        h = x * (1.0 + sc1) + sh1
        hc = ctx * (1.0 + csc1) + csh1
        kv_ctx = _kv_side(hc @ w_kv, lp, None)
        kv_lat = _kv_side(h @ w_kv, lp, rope)
        y = _token_mixer(h @ w_rest, kv_lat, kv_ctx, rope, lp, lam, lam_init)
        x = _layernorm(DEEPNORM_ALPHA * x + g1 * y, ln1_g[l], ln1_b[l])
        if not last:
            yc = _token_mixer(hc @ w_rest, kv_ctx, None, None, lp, lam, lam_init)
            ctx = _layernorm(DEEPNORM_ALPHA * ctx + cg1 * yc, ln1_g[l], ln1_b[l])
        h = x * (1.0 + sc2) + sh2
        if last:
            y = _moe(h, router_w, router_bias, moe_w1[l], moe_w3[l], moe_w2[l])
            x = _layernorm(DEEPNORM_ALPHA * x + g2 * y, ln2_g[l], ln2_b[l])
        else:
            hc = ctx * (1.0 + csc2) + csh2
            y_all = _moe(jnp.concatenate([hc, h], axis=1), router_w, router_bias, moe_w1[l], moe_w3[l], moe_w2[l])
            ctx = _layernorm(DEEPNORM_ALPHA * ctx + cg2 * y_all[:, :n_ctx], ln2_g[l], ln2_b[l])
            x = _layernorm(DEEPNORM_ALPHA * x + g2 * y_all[:, n_ctx:], ln2_g[l], ln2_b[l])
    return x
```

```python
import functools
import math

import jax
import jax.numpy as jnp
from jax import lax
from jax.experimental import pallas as pl
from jax.experimental.pallas import tpu as pltpu

F32 = jnp.float32
BF16 = jnp.bfloat16
U32 = jnp.uint32
I32 = jnp.int32

LANES = 128
SUBLANES = 8
TILE_M = 256
VMEM_LIMIT = 56 << 20

GRID_W = 64
ROPE_DIM = 64
ROPE_BASE = 10000.0
MLA_HEADS = 4
MLA_NOPE = 128
MLA_ROPE = 64
MLA_V = 128
MLA_Q_RANK = 384
MLA_KV_RANK = 256
MLA_HEAD_PAD = 256
DIFF_HEADS = 4
DIFF_QK = 64
DIFF_V = 128
CHUNK = 128
GMLP_GROUPS = 4
GMLP_W = 512
POOL_WINDOWS = (2, 4, 8, 16)
POOL_PAD = 16
N_BRANCH = 4
BRANCH_W = 512
N_EXPERTS = 32
N_GROUPS = 4
EXPERTS_PER_GROUP = 8
D_EXPERT = 512
LN_EPS = 1e-5
RMS_EPS = 1e-6

SEG_DK, SEG_DV, SEG_DQ, SEG_PL, SEG_GM, SEG_CKV, SEG_KPE, SEG_CQ = 0, 512, 1024, 1536, 2048, 3072, 3328, 3456
N_SMALL = 3840


def _params(*sem):
    return pltpu.CompilerParams(dimension_semantics=sem, vmem_limit_bytes=VMEM_LIMIT)


def _const_spec(shape):
    nd = len(shape)
    return pl.BlockSpec(shape, lambda *_: (0,) * nd, pipeline_mode=pl.Buffered(1))


def _sigmoid(x):
    return 1.0 / (1.0 + jnp.exp(-x))


def _layernorm(x, g, b):
    mu = jnp.mean(x, axis=-1, keepdims=True)
    xc = x - mu
    var = jnp.mean(xc * xc, axis=-1, keepdims=True)
    return xc * lax.rsqrt(var + LN_EPS) * g + b


def _rmsnorm(x, g):
    return x * lax.rsqrt(jnp.mean(x * x, axis=-1, keepdims=True) + RMS_EPS) * g


def _rope(x, cos_t, sin_t):
    lane = lax.broadcasted_iota(I32, (x.shape[0], LANES), 1)
    first_half = (lane % ROPE_DIM) < (ROPE_DIM // 2)
    outs = []
    for j in range(x.shape[1] // LANES):
        xc = x[:, j * LANES:(j + 1) * LANES]
        partner = jnp.where(first_half, pltpu.roll(xc, LANES - ROPE_DIM // 2, 1), pltpu.roll(xc, ROPE_DIM // 2, 1))
        outs.append(xc * cos_t + partner * sin_t)
    return outs[0] if len(outs) == 1 else jnp.concatenate(outs, axis=1)


def _ada_kernel(c_ref, w_ref, b_ref, o_ref):
    c = c_ref[...]
    s = (c * _sigmoid(c)).astype(BF16)
    o_ref[0] = jnp.dot(s, w_ref[0].astype(BF16), preferred_element_type=F32) + b_ref[0]


def _ada(c_all, ada_w, ada_b):
    depth, d, n6 = ada_w.shape
    tn = 1024
    rows = c_all.shape[0]
    return pl.pallas_call(
        _ada_kernel,
        grid=(depth, n6 // tn),
        in_specs=[pl.BlockSpec((rows, d), lambda l, j: (0, 0)),
                  pl.BlockSpec((1, d, tn), lambda l, j: (l, 0, j)),
                  pl.BlockSpec((1, 1, tn), lambda l, j: (l, 0, j))],
        out_specs=pl.BlockSpec((1, rows, tn), lambda l, j: (l, 0, j)),
        out_shape=jax.ShapeDtypeStruct((depth, rows, n6), F32),
        compiler_params=_params("parallel", "parallel"),
        name="ada",
    )(c_all, ada_w, ada_b.reshape(depth, 1, n6))


def _inproj_kernel(x_ref, sc_ref, sh_ref, w_ref, o_ref):
    h = (x_ref[0] * (1.0 + sc_ref[0]) + sh_ref[0]).astype(BF16)
    o_ref[0] = jnp.dot(h, w_ref[...], preferred_element_type=F32).astype(o_ref.dtype)


def _mod_spec(n_ctx_tiles, d, order):
    if order == "jbq":
        return pl.BlockSpec((1, 1, d), lambda j, b, q: (2 * b + (q >= n_ctx_tiles).astype(I32), 0, 0))
    return pl.BlockSpec((1, 1, d), lambda b, q: (2 * b + (q >= n_ctx_tiles).astype(I32), 0, 0))


def _inproj(x, sc, sh, w, tn, n_ctx_tiles, name):
    bsz, l_all, d = x.shape
    n = w.shape[1]
    nt = l_all // TILE_M
    return pl.pallas_call(
        _inproj_kernel,
        grid=(n // tn, bsz, nt),
        in_specs=[pl.BlockSpec((1, TILE_M, d), lambda j, b, q: (b, q, 0)),
                  _mod_spec(n_ctx_tiles, d, "jbq"), _mod_spec(n_ctx_tiles, d, "jbq"),
                  pl.BlockSpec((d, tn), lambda j, b, q: (0, j))],
        out_specs=pl.BlockSpec((1, TILE_M, tn), lambda j, b, q: (b, q, j)),
        out_shape=jax.ShapeDtypeStruct((bsz, l_all, n), BF16),
        compiler_params=_params("parallel", "parallel", "parallel"),
        name=name,
    )(x, sc, sh, w)


def _prep_kernel(ckv_ref, kpe_ref, cq_ref, dq_ref, dk_ref, cos_ref, sin_ref, gkv_ref, gq_ref, wukv_ref, wuq_ref,
                 mq_ref, mk_ref, mv_ref, dqr_ref, dkr_ref, *, mla_scale, diff_scale):
    cos_t = cos_ref[...]
    sin_t = sin_ref[...]
    ckv = _rmsnorm(ckv_ref[0].astype(F32), gkv_ref[...]).astype(BF16)
    kv = jnp.dot(ckv, wukv_ref[...], preferred_element_type=F32)
    kpe = _rope(kpe_ref[0].astype(F32), cos_t, sin_t).astype(BF16)
    for h in range(MLA_HEADS):
        base = h * MLA_HEAD_PAD
        mk_ref[0, :, base:base + MLA_NOPE] = kv[:, h * MLA_NOPE:(h + 1) * MLA_NOPE].astype(BF16)
        mk_ref[0, :, base + MLA_NOPE:base + MLA_HEAD_PAD] = kpe
    mv_ref[0] = kv[:, MLA_HEADS * MLA_NOPE:].astype(BF16)
    cq = _rmsnorm(cq_ref[0].astype(F32), gq_ref[...]).astype(BF16)
    q = jnp.dot(cq, wuq_ref[...], preferred_element_type=F32)
    for h in range(MLA_HEADS):
        base = h * MLA_HEAD_PAD
        mq_ref[0, :, base:base + MLA_NOPE] = (q[:, base:base + MLA_NOPE] * mla_scale).astype(BF16)
        qpe = _rope(q[:, base + MLA_NOPE:base + MLA_HEAD_PAD], cos_t, sin_t)
        mq_ref[0, :, base + MLA_NOPE:base + MLA_HEAD_PAD] = (qpe * mla_scale).astype(BF16)
    dqr_ref[0] = (_rope(dq_ref[0].astype(F32), cos_t, sin_t) * diff_scale).astype(BF16)
    dkr_ref[0] = _rope(dk_ref[0].astype(F32), cos_t, sin_t).astype(BF16)


def _prep(p, cos_t, sin_t, g_kv, g_q, w_ukv, w_uq):
    bsz, l_all, _ = p.shape
    nt = l_all // TILE_M

    def seg(width, offset):
        return pl.BlockSpec((1, TILE_M, width), lambda b, q: (b, q, offset // width))

    def out(width):
        return pl.BlockSpec((1, TILE_M, width), lambda b, q: (b, q, 0))

    kern = functools.partial(_prep_kernel, mla_scale=(MLA_NOPE + MLA_ROPE) ** -0.5, diff_scale=DIFF_QK ** -0.5)
    widths = (MLA_HEADS * MLA_HEAD_PAD, MLA_HEADS * MLA_HEAD_PAD, MLA_HEADS * MLA_V, 512, 512)
    return pl.pallas_call(
        kern,
        grid=(bsz, nt),
        in_specs=[seg(MLA_KV_RANK, SEG_CKV), seg(LANES, SEG_KPE), seg(MLA_Q_RANK, SEG_CQ), seg(512, SEG_DQ),
                  seg(512, SEG_DK),
                  pl.BlockSpec((TILE_M, LANES), lambda b, q: (q, 0)), pl.BlockSpec((TILE_M, LANES), lambda b, q: (q, 0)),
                  _const_spec(g_kv.shape), _const_spec(g_q.shape), _const_spec(w_ukv.shape), _const_spec(w_uq.shape)],
        out_specs=[out(w) for w in widths],
        out_shape=[jax.ShapeDtypeStruct((bsz, l_all, w), BF16) for w in widths],
        compiler_params=_params("parallel", "parallel"),
        name="attn_prep",
    )(p, p, p, p, p, cos_t, sin_t, g_kv, g_q, w_ukv, w_uq)


def _softmax_parts(s):
    m = jnp.max(s, axis=-1, keepdims=True)
    p = jnp.exp(s - m)
    return p, jnp.sum(p, axis=-1, keepdims=True)


def _qk(q, k):
    return lax.dot_general(q, k, (((1,), (1,)), ((), ())), preferred_element_type=F32)


def _mla_kernel(q_ref, k_ref, v_ref, o_ref, *, n_ctx):
    q = q_ref[0]

    def run(nk):
        p, l = _softmax_parts(_qk(q, k_ref[0, :nk, :]))
        o = jnp.dot(p.astype(BF16), v_ref[0, :nk, :], preferred_element_type=F32) / l
        o_ref[0] = o.astype(o_ref.dtype)

    is_ctx = pl.program_id(2) < n_ctx // TILE_M
    pl.when(is_ctx)(lambda: run(n_ctx))
    pl.when(jnp.logical_not(is_ctx))(lambda: run(k_ref.shape[1]))


def _mla_attention(mq, mk, mv, n_ctx):
    bsz, l_all, _ = mq.shape
    nt = l_all // TILE_M
    return pl.pallas_call(
        functools.partial(_mla_kernel, n_ctx=n_ctx),
        grid=(bsz, MLA_HEADS, nt),
        in_specs=[pl.BlockSpec((1, TILE_M, MLA_HEAD_PAD), lambda b, h, q: (b, q, h)),
                  pl.BlockSpec((1, l_all, MLA_HEAD_PAD), lambda b, h, q: (b, 0, h)),
                  pl.BlockSpec((1, l_all, MLA_V), lambda b, h, q: (b, 0, h))],
        out_specs=pl.BlockSpec((1, TILE_M, MLA_V), lambda b, h, q: (b, q, h)),
        out_shape=jax.ShapeDtypeStruct((bsz, l_all, MLA_HEADS * MLA_V), BF16),
        compiler_params=_params("parallel", "parallel", "parallel"),
        name="mla_attn",
    )(mq, mk, mv)


def _diff_kernel(q_ref, k_ref, v_ref, lam_ref, g_ref, o_ref, *, n_ctx, lam_init):
    lv = lam_ref[...]
    lam = (jnp.exp(jnp.sum(lv[0:1] * lv[1:2], axis=-1, keepdims=True))
           - jnp.exp(jnp.sum(lv[2:3] * lv[3:4], axis=-1, keepdims=True)) + lam_init)
    q = q_ref[0]
    lane = lax.broadcasted_iota(I32, q.shape, 1)
    q1 = jnp.where(lane < DIFF_QK, q, jnp.zeros_like(q))
    q2 = jnp.where(lane >= DIFF_QK, q, jnp.zeros_like(q))

    def run(nk):
        k = k_ref[0, :nk, :]
        p1, l1 = _softmax_parts(_qk(q1, k))
        p2, l2 = _softmax_parts(_qk(q2, k))
        w = p1 * (1.0 / l1) - p2 * (lam / l2)
        o = jnp.dot(w.astype(BF16), v_ref[0, :nk, :], preferred_element_type=F32)
        o_ref[0] = (_rmsnorm(o, g_ref[...]) * (1.0 - lam_init)).astype(o_ref.dtype)

    is_ctx = pl.program_id(2) < n_ctx // TILE_M
    pl.when(is_ctx)(lambda: run(n_ctx))
    pl.when(jnp.logical_not(is_ctx))(lambda: run(k_ref.shape[1]))


def _diff_attention(dqr, dkr, p, lam_vec, g_sub, n_ctx, lam_init):
    bsz, l_all, _ = dqr.shape
    nt = l_all // TILE_M
    w = 2 * DIFF_QK
    return pl.pallas_call(
        functools.partial(_diff_kernel, n_ctx=n_ctx, lam_init=lam_init),
        grid=(bsz, DIFF_HEADS, nt),
        in_specs=[pl.BlockSpec((1, TILE_M, w), lambda b, h, q: (b, q, h)),
                  pl.BlockSpec((1, l_all, w), lambda b, h, q: (b, 0, h)),
                  pl.BlockSpec((1, l_all, DIFF_V), lambda b, h, q: (b, 0, SEG_DV // DIFF_V + h)),
                  _const_spec(lam_vec.shape), _const_spec(g_sub.shape)],
        out_specs=pl.BlockSpec((1, TILE_M, DIFF_V), lambda b, h, q: (b, q, h)),
        out_shape=jax.ShapeDtypeStruct((bsz, l_all, DIFF_HEADS * DIFF_V), BF16),
        compiler_params=_params("parallel", "parallel", "parallel"),
        name="diff_attn",
    )(dqr, dkr, p, lam_vec, g_sub)


def _gelu_tanh(x):
    return 0.5 * x * (1.0 + jnp.tanh(math.sqrt(2.0 / math.pi) * (x + 0.044715 * (x * x * x))))


def _gmlp_kernel(z_ref, g_ref, b_ref, ws_ref, bst_ref, o_ref):
    z = _gelu_tanh(z_ref[0].astype(F32))
    u = z[:, :GMLP_W]
    v = _layernorm(z[:, GMLP_W:], g_ref[...], b_ref[...]).astype(BF16)
    gw = GMLP_W // GMLP_GROUPS
    for c in range(z.shape[0] // CHUNK):
        rows = slice(c * CHUNK, (c + 1) * CHUNK)
        for g in range(GMLP_GROUPS):
            cols = slice(g * gw, (g + 1) * gw)
            s = jnp.dot(ws_ref[g], v[rows, cols], preferred_element_type=F32) + bst_ref[:, g:g + 1]
            o_ref[0, rows, cols] = (u[rows, cols] * s).astype(o_ref.dtype)


def _gmlp(p, ln_g, ln_b, ws, bs_t):
    bsz, l_all, _ = p.shape
    nt = l_all // TILE_M
    return pl.pallas_call(
        _gmlp_kernel,
        grid=(bsz, nt),
        in_specs=[pl.BlockSpec((1, TILE_M, 2 * GMLP_W), lambda b, q: (b, q, SEG_GM // (2 * GMLP_W))),
                  _const_spec(ln_g.shape), _const_spec(ln_b.shape), _const_spec(ws.shape), _const_spec(bs_t.shape)],
        out_specs=pl.BlockSpec((1, TILE_M, GMLP_W), lambda b, q: (b, q, 0)),
        out_shape=jax.ShapeDtypeStruct((bsz, l_all, GMLP_W), BF16),
        compiler_params=_params("parallel", "parallel"),
        name="gmlp",
    )(p, ln_g, ln_b, ws, bs_t)


def _pool_kernel(p_ref, w_ref, sc_ref, o_ref, pad_ref, *, n_ctx):
    n_all = p_ref.shape[1]
    gw = LANES
    t = lax.broadcasted_iota(I32, (n_all, gw), 0)
    seq_lo = jnp.where(t < n_ctx, 0, n_ctx)
    seq_hi = jnp.where(t < n_ctx, n_ctx, n_all)
    pad_ref[0:POOL_PAD, :] = jnp.zeros((POOL_PAD, gw), F32)
    pad_ref[POOL_PAD + n_all:, :] = jnp.zeros((POOL_PAD, gw), F32)
    for i, win in enumerate(POOL_WINDOWS):
        cols = slice(i * gw, (i + 1) * gw)
        x = p_ref[0, :, cols].astype(F32)
        pad_ref[POOL_PAD:POOL_PAD + n_all, :] = x
        acc = jnp.zeros((n_all, gw), F32)
        cnt = jnp.zeros((n_all, gw), F32)
        for d in range(-(win // 2), win - win // 2):
            valid = jnp.logical_and(t + d >= seq_lo, t + d < seq_hi)
            acc = acc + jnp.where(valid, pad_ref[POOL_PAD + d:POOL_PAD + d + n_all, :], 0.0)
            cnt = cnt + valid.astype(F32)
        resid = (acc / cnt - x).astype(BF16)
        o = jnp.dot(resid, w_ref[i], preferred_element_type=F32) * sc_ref[:, cols]
        o_ref[0, :, cols] = o.astype(o_ref.dtype)


def _pool(p, pool_w, pool_scale, n_ctx):
    bsz, l_all, _ = p.shape
    width = len(POOL_WINDOWS) * LANES
    return pl.pallas_call(
        functools.partial(_pool_kernel, n_ctx=n_ctx),
        grid=(bsz,),
        in_specs=[pl.BlockSpec((1, l_all, width), lambda b: (b, 0, SEG_PL // width)),
                  _const_spec(pool_w.shape), _const_spec(pool_scale.shape)],
        out_specs=pl.BlockSpec((1, l_all, width), lambda b: (b, 0, 0)),
        out_shape=jax.ShapeDtypeStruct((bsz, l_all, width), BF16),
        scratch_shapes=[pltpu.VMEM((l_all + 2 * POOL_PAD, LANES), F32)],
        compiler_params=_params("parallel"),
        name="pool",
    )(p, pool_w, pool_scale)


def _merge_kernel(b0_ref, b1_ref, b2_ref, b3_ref, gl_ref, x_ref, g1_ref, wb_ref, wo_ref, lg_ref, lb_ref, o_ref, *,
                  alpha):
    d = x_ref.shape[2]
    merged = None
    for n, br in enumerate((b0_ref, b1_ref, b2_ref, b3_ref)):
        t = jnp.dot(br[0], wb_ref[n], preferred_element_type=F32)
        term = _sigmoid(gl_ref[0, :, n * d:(n + 1) * d].astype(F32)) * t
        merged = term if merged is None else merged + term
    y = jnp.dot(merged.astype(BF16), wo_ref[...], preferred_element_type=F32)
    o_ref[0] = _layernorm(alpha * x_ref[0] + g1_ref[0] * y, lg_ref[...], lb_ref[...])


def _merge(branches, gl, x, g1, w_branch, w_out, ln_g, ln_b, n_ctx_tiles, alpha):
    bsz, l_all, d = x.shape
    nt = l_all // TILE_M
    row = lambda width: pl.BlockSpec((1, TILE_M, width), lambda b, q: (b, q, 0))
    return pl.pallas_call(
        functools.partial(_merge_kernel, alpha=alpha),
        grid=(bsz, nt),
        in_specs=[row(BRANCH_W)] * N_BRANCH + [row(N_BRANCH * d), row(d), _mod_spec(n_ctx_tiles, d, "bq"),
                                               _const_spec(w_branch.shape), _const_spec(w_out.shape),
                                               _const_spec(ln_g.shape), _const_spec(ln_b.shape)],
        out_specs=row(d),
        out_shape=jax.ShapeDtypeStruct((bsz, l_all, d), F32),
        compiler_params=_params("parallel", "parallel"),
        name="merge",
    )(*branches, gl, x, g1, w_branch, w_out, ln_g, ln_b)


def _pack_rows(ref, y):
    m, d = y.shape
    bits = pltpu.bitcast(y.astype(BF16).astype(F32), U32)
    words = (bits[:, :d // 2] >> 16) | bits[:, d // 2:]
    for s in range(SUBLANES):
        ref[pl.ds(s, m, stride=SUBLANES), :] = words[:, s * LANES:(s + 1) * LANES]


def _unpack_rows(ref, m):
    lo, hi = [], []
    for s in range(SUBLANES):
        w = ref[pl.ds(s, m, stride=SUBLANES), :]
        lo.append(pltpu.bitcast(w << 16, F32))
        hi.append(pltpu.bitcast(w & jnp.uint32(0xFFFF0000), F32))
    return lo, hi


def _route_kernel(x_ref, sc_ref, sh_ref, rw_ref, rb_ref, hp_ref, idx_ref, wt_ref):
    h = x_ref[0] * (1.0 + sc_ref[0]) + sh_ref[0]
    logits = jnp.dot(h, rw_ref[...], precision=lax.Precision.HIGHEST, preferred_element_type=F32)
    scores = _sigmoid(logits)
    biased = scores + rb_ref[...]
    lane = lax.broadcasted_iota(I32, biased.shape, 1)
    neg = jnp.float32(-jnp.inf)
    best = None
    for g in range(N_GROUPS):
        in_g = jnp.logical_and(lane >= g * EXPERTS_PER_GROUP, lane < (g + 1) * EXPERTS_PER_GROUP)
        a = jnp.where(in_g, biased, neg)
        m1 = jnp.max(a, axis=-1, keepdims=True)
        i1 = jnp.min(jnp.where(a == m1, lane, LANES), axis=-1, keepdims=True)
        a2 = jnp.where(lane == i1, neg, a)
        m2 = jnp.max(a2, axis=-1, keepdims=True)
        i2 = jnp.min(jnp.where(a2 == m2, lane, LANES), axis=-1, keepdims=True)
        gs = m1 + m2
        if best is None:
            best = (gs, i1, i2)
        else:
            take = gs > best[0]
            best = (jnp.where(take, gs, best[0]), jnp.where(take, i1, best[1]), jnp.where(take, i2, best[2]))
    _, e0, e1 = best
    w0 = jnp.sum(jnp.where(lane == e0, scores, 0.0), axis=-1, keepdims=True)
    w1 = jnp.sum(jnp.where(lane == e1, scores, 0.0), axis=-1, keepdims=True)
    tot = w0 + w1
    idx_ref[...] = jnp.where(lane == 0, e0, jnp.where(lane == 1, e1, 0))
    wt_ref[...] = jnp.where(lane == 0, w0 / tot, jnp.where(lane == 1, w1 / tot, 0.0))
    _pack_rows(hp_ref, h)


def _route(x, sc, sh, rw, rb, n_ctx_tiles):
    bsz, l_all, d = x.shape
    nt = l_all // TILE_M
    t_all = bsz * l_all
    flat = lambda b, q: (b * nt + q, 0)
    return pl.pallas_call(
        _route_kernel,
        grid=(bsz, nt),
        in_specs=[pl.BlockSpec((1, TILE_M, d), lambda b, q: (b, q, 0)),
                  _mod_spec(n_ctx_tiles, d, "bq"), _mod_spec(n_ctx_tiles, d, "bq"),
                  _const_spec(rw.shape), _const_spec(rb.shape)],
        out_specs=[pl.BlockSpec((TILE_M * SUBLANES, LANES), flat), pl.BlockSpec((TILE_M, LANES), flat),
                   pl.BlockSpec((TILE_M, LANES), flat)],
        out_shape=[jax.ShapeDtypeStruct((t_all * SUBLANES, LANES), U32), jax.ShapeDtypeStruct((t_all, LANES), I32),
                   jax.ShapeDtypeStruct((t_all, LANES), F32)],
        compiler_params=_params("parallel", "parallel"),
        name="moe_route",
    )(x, sc, sh, rw, rb)


def _route_tables(e_pairs, t_all, n_tiles):
    onehot = (e_pairs[:, None] == jnp.arange(N_EXPERTS, dtype=I32)[None, :]).astype(I32)
    csum = jnp.cumsum(onehot, axis=0)
    rank = jnp.sum((csum - onehot) * onehot, axis=1)
    counts = csum[-1]
    padded = ((counts + TILE_M - 1) // TILE_M) * TILE_M
    ends = jnp.cumsum(padded)
    pos = (ends - padded)[e_pairs] + rank
    tok = jnp.tile(jnp.arange(t_all, dtype=I32), 2)
    src_tok = jnp.zeros((n_tiles * TILE_M,), I32).at[pos].set(tok)
    tile_start = jnp.arange(n_tiles, dtype=I32) * TILE_M
    tile_expert = jnp.minimum(jnp.searchsorted(ends, tile_start, side="right").astype(I32), N_EXPERTS - 1)
    tile_valid = (tile_start < ends[-1]).astype(I32)
    return src_tok, pos.astype(I32), tile_expert, tile_valid


def _gather_kernel(idx_ref, src_ref, dst_ref, sem):
    i = pl.program_id(0)
    slot = i % 2

    def issue(r, carry):
        pltpu.make_async_copy(src_ref.at[idx_ref[0, 0, r]], dst_ref.at[i * TILE_M + r], sem.at[slot]).start()
        return carry

    lax.fori_loop(0, TILE_M, issue, 0, unroll=8)

    def wait_tile(j, s):
        rows = dst_ref.at[pl.ds(j * TILE_M, TILE_M)]
        pltpu.make_async_copy(rows, rows, sem.at[s]).wait()

    pl.when(i > 0)(lambda: wait_tile(i - 1, 1 - slot))
    pl.when(i == pl.num_programs(0) - 1)(lambda: wait_tile(i, slot))


def _gather_rows(src, idx):
    m = idx.shape[0]
    nt = m // TILE_M
    return pl.pallas_call(
        _gather_kernel,
        grid=(nt,),
        in_specs=[pl.BlockSpec((1, 1, TILE_M), lambda i: (i, 0, 0), memory_space=pltpu.SMEM),
                  pl.BlockSpec(memory_space=pl.ANY)],
        out_specs=pl.BlockSpec(memory_space=pl.ANY),
        out_shape=jax.ShapeDtypeStruct((m,) + src.shape[1:], src.dtype),
        scratch_shapes=[pltpu.SemaphoreType.DMA((2,))],
        compiler_params=_params("arbitrary"),
        name="moe_gather",
    )(idx.reshape(nt, 1, TILE_M), src)


def _expert_kernel(te_ref, tv_ref, x_ref, w1_ref, w3_ref, w2_ref, o_ref, xb_ref):
    del te_ref
    valid = tv_ref[pl.program_id(0)] == 1

    @pl.when(valid)
    def _():
        half = xb_ref.shape[1] // 2
        lo, hi = _unpack_rows(x_ref, TILE_M)
        for s in range(SUBLANES):
            xb_ref[:, s * LANES:(s + 1) * LANES] = lo[s].astype(BF16)
            xb_ref[:, half + s * LANES:half + (s + 1) * LANES] = hi[s].astype(BF16)
        x = xb_ref[...]
        a = jnp.dot(x, w1_ref[0], preferred_element_type=F32)
        b = jnp.dot(x, w3_ref[0], preferred_element_type=F32)
        hid = (a * _sigmoid(a) * b).astype(BF16)
        _pack_rows(o_ref, jnp.dot(hid, w2_ref[0], preferred_element_type=F32))

    @pl.when(jnp.logical_not(valid))
    def _():
        o_ref[...] = jnp.zeros(o_ref.shape, o_ref.dtype)


def _experts(xs, tile_expert, tile_valid, w1, w3, w2):
    n_tiles = tile_expert.shape[0]
    _, d, de = w1.shape
    blk = TILE_M * SUBLANES
    return pl.pallas_call(
        _expert_kernel,
        grid_spec=pltpu.PrefetchScalarGridSpec(
            num_scalar_prefetch=2,
            grid=(n_tiles,),
            in_specs=[pl.BlockSpec((blk, LANES), lambda i, te, tv: (i, 0)),
                      pl.BlockSpec((1, d, de), lambda i, te, tv: (te[i], 0, 0)),
                      pl.BlockSpec((1, d, de), lambda i, te, tv: (te[i], 0, 0)),
                      pl.BlockSpec((1, de, d), lambda i, te, tv: (te[i], 0, 0))],
            out_specs=pl.BlockSpec((blk, LANES), lambda i, te, tv: (i, 0)),
            scratch_shapes=[pltpu.VMEM((TILE_M, d), BF16)]),
        out_shape=jax.ShapeDtypeStruct(xs.shape, U32),
        compiler_params=_params("arbitrary"),
        name="moe_experts",
    )(tile_expert, tile_valid, xs, w1, w3, w2)


def _combine_kernel(y0_ref, y1_ref, wt_ref, x_ref, g2_ref, lg_ref, lb_ref, o_ref, *, alpha):
    w = wt_ref[...]
    lo0, hi0 = _unpack_rows(y0_ref, TILE_M)
    lo1, hi1 = _unpack_rows(y1_ref, TILE_M)
    y = w[:, 0:1] * jnp.concatenate(lo0 + hi0, axis=1) + w[:, 1:2] * jnp.concatenate(lo1 + hi1, axis=1)
    o_ref[0] = _layernorm(alpha * x_ref[0] + g2_ref[0] * y, lg_ref[...], lb_ref[...])


def _combine(yg, wt, x, g2, ln_g, ln_b, n_ctx_tiles, alpha):
    bsz, l_all, d = x.shape
    nt = l_all // TILE_M
    blk = TILE_M * SUBLANES
    return pl.pallas_call(
        functools.partial(_combine_kernel, alpha=alpha),
        grid=(bsz, nt),
        in_specs=[pl.BlockSpec((blk, LANES), lambda b, q: (b * nt + q, 0)),
                  pl.BlockSpec((blk, LANES), lambda b, q: (bsz * nt + b * nt + q, 0)),
                  pl.BlockSpec((TILE_M, LANES), lambda b, q: (b * nt + q, 0)),
                  pl.BlockSpec((1, TILE_M, d), lambda b, q: (b, q, 0)),
                  _mod_spec(n_ctx_tiles, d, "bq"), _const_spec(ln_g.shape), _const_spec(ln_b.shape)],
        out_specs=pl.BlockSpec((1, TILE_M, d), lambda b, q: (b, q, 0)),
        out_shape=jax.ShapeDtypeStruct((bsz, l_all, d), F32),
        compiler_params=_params("parallel", "parallel"),
        name="moe_combine",
    )(yg, yg, wt, x, g2, ln_g, ln_b)


def _rope_tables(n_ctx, n_lat):
    rows = n_lat // GRID_W
    row = jnp.repeat(jnp.arange(rows, dtype=F32), GRID_W)
    col = jnp.tile(jnp.arange(GRID_W, dtype=F32), rows)
    n_freq = ROPE_DIM // 4
    inv = ROPE_BASE ** (-jnp.arange(n_freq, dtype=F32) / n_freq)
    ang = jnp.concatenate([row[:, None] * inv, col[:, None] * inv], axis=-1)
    cos = jnp.concatenate([jnp.ones((n_ctx, ROPE_DIM // 2), F32), jnp.cos(ang)], axis=0)
    sin = jnp.concatenate([jnp.zeros((n_ctx, ROPE_DIM // 2), F32), jnp.sin(ang)], axis=0)
    reps = LANES // ROPE_DIM
    return jnp.tile(jnp.concatenate([cos, cos], axis=1), (1, reps)), jnp.tile(jnp.concatenate([-sin, sin], axis=1), (1, reps))


def _split_w_in(w):
    d = w.shape[0]
    o = 0
    segs = {}
    for name, width in (("ckv", MLA_KV_RANK), ("kpe", MLA_ROPE), ("dk", 512), ("dv", 512), ("cq", MLA_Q_RANK),
                        ("dq", 512), ("gm", 2 * GMLP_W), ("pl", 512), ("gate", N_BRANCH * d)):
        segs[name] = w[:, o:o + width]
        o += width
    small = jnp.concatenate([segs["dk"], segs["dv"], segs["dq"], segs["pl"], segs["gm"], segs["ckv"], segs["kpe"],
                             jnp.zeros((d, LANES - MLA_ROPE), w.dtype), segs["cq"]], axis=1)
    return small.astype(BF16), segs["gate"].astype(BF16)


def _forward(x, c, ctx, c_ctx, ada_w, ada_b, w_in, mla_q_norm, mla_kv_norm, mla_w_uq, mla_w_ukv, diff_lambda,
             diff_subln, gmlp_ln_g, gmlp_ln_b, gmlp_ws, gmlp_bs, pool_w, pool_scale, w_branch, w_out, ln1_g, ln1_b,
             ln2_g, ln2_b, router_w, router_bias, moe_w1, moe_w3, moe_w2):
    bsz, n_lat, d = x.shape
    n_ctx = ctx.shape[1]
    depth = w_in.shape[0]
    l_all = n_ctx + n_lat
    t_all = bsz * l_all
    n_ctx_tiles = n_ctx // TILE_M
    alpha = (2 * depth) ** 0.25
    n_row_tiles = (2 * t_all) // TILE_M + N_EXPERTS

    xs = jnp.concatenate([ctx, x], axis=1)
    cos_t, sin_t = _rope_tables(n_ctx, n_lat)

    c_rows = ((bsz + 1 + SUBLANES - 1) // SUBLANES) * SUBLANES
    c_all = jnp.zeros((c_rows, d), F32).at[:bsz].set(c).at[bsz].set(c_ctx)
    mod = _ada(c_all, ada_w, ada_b)
    pick = jnp.stack([jnp.full((bsz,), bsz, I32), jnp.arange(bsz, dtype=I32)], axis=1).reshape(-1)
    mod = mod[:, pick].reshape(depth, 2 * bsz, 1, 6, d)

    rw = jnp.zeros((d, LANES), F32).at[:, :N_EXPERTS].set(router_w)
    rb = jnp.zeros((1, LANES), F32).at[0, :N_EXPERTS].set(router_bias)

    for l in range(depth):
        sh1, sc1, g1, sh2, sc2, g2 = (mod[l, :, :, k] for k in range(6))
        lam_init = 0.8 - 0.6 * math.exp(-0.3 * l)
        w_small, w_gate = _split_w_in(w_in[l])
        w_ukv = mla_w_ukv[l].reshape(MLA_KV_RANK, MLA_HEADS, 2, MLA_NOPE).transpose(0, 2, 1, 3).reshape(MLA_KV_RANK, -1)
        w_uq = jnp.pad(mla_w_uq[l].reshape(MLA_Q_RANK, MLA_HEADS, MLA_NOPE + MLA_ROPE),
                       ((0, 0), (0, 0), (0, MLA_HEAD_PAD - MLA_NOPE - MLA_ROPE))).reshape(MLA_Q_RANK, -1)

        p = _inproj(xs, sc1, sh1, w_small, 1280, n_ctx_tiles, "inproj_small")
        gl = _inproj(xs, sc1, sh1, w_gate, 2048, n_ctx_tiles, "inproj_gate")
        mq, mk, mv, dqr, dkr = _prep(p, cos_t, sin_t, mla_kv_norm[l][None], mla_q_norm[l][None],
                                     w_ukv.astype(BF16), w_uq.astype(BF16))
        mla_out = _mla_attention(mq, mk, mv, n_ctx)
        d_out = _diff_attention(dqr, dkr, p, diff_lambda[l], diff_subln[l][None], n_ctx, lam_init)
        g_out = _gmlp(p, gmlp_ln_g[l][None], gmlp_ln_b[l][None], gmlp_ws[l].astype(BF16), gmlp_bs[l].T)
        p_out = _pool(p, pool_w[l].astype(BF16), pool_scale[l][None], n_ctx)
        xs = _merge((mla_out, d_out, g_out, p_out), gl, xs, g1, w_branch[l].astype(BF16), w_out[l].astype(BF16),
                    ln1_g[l][None], ln1_b[l][None], n_ctx_tiles, alpha)

        hp, idx, wt = _route(xs, sc2, sh2, rw, rb, n_ctx_tiles)
        e_pairs = idx[:, :2].T.reshape(-1)
        src_tok, pos, tile_expert, tile_valid = _route_tables(e_pairs, t_all, n_row_tiles)
        rows = _gather_rows(hp.reshape(t_all, SUBLANES, LANES), src_tok)
        ys = _experts(rows.reshape(-1, LANES), tile_expert, tile_valid, moe_w1[l].astype(BF16), moe_w3[l].astype(BF16),
                      moe_w2[l].astype(BF16))
        yg = _gather_rows(ys.reshape(-1, SUBLANES, LANES), pos)
        xs = _combine(yg.reshape(-1, LANES), wt, xs, g2, ln2_g[l][None], ln2_b[l][None], n_ctx_tiles, alpha)
    return xs[:, n_ctx:]


_forward_jit = jax.jit(_forward)


def kernel(x, c, ctx, c_ctx, ada_w, ada_b, w_in, mla_q_norm, mla_kv_norm, mla_w_uq, mla_w_ukv, diff_lambda, diff_subln, gmlp_ln_g, gmlp_ln_b, gmlp_ws, gmlp_bs, pool_w, pool_scale, w_branch, w_out, ln1_g, ln1_b, ln2_g, ln2_b, router_w, router_bias, moe_w1, moe_w3, moe_w2):
    return _forward_jit(x, c, ctx, c_ctx, ada_w, ada_b, w_in, mla_q_norm, mla_kv_norm, mla_w_uq, mla_w_ukv,
                        diff_lambda, diff_subln, gmlp_ln_g, gmlp_ln_b, gmlp_ws, gmlp_bs, pool_w, pool_scale,
                        w_branch, w_out, ln1_g, ln1_b, ln2_g, ln2_b, router_w, router_bias, moe_w1, moe_w3, moe_w2)
```

```python
import functools
import math

import jax
import jax.numpy as jnp
from jax import lax
from jax.experimental import pallas as pl
from jax.experimental.pallas import tpu as pltpu

F32 = jnp.float32
BF16 = jnp.bfloat16
U32 = jnp.uint32
I32 = jnp.int32

LANES = 128
SUBLANES = 8
TILE_M = 256
VMEM_LIMIT = 56 << 20

GRID_W = 64
ROPE_DIM = 64
ROPE_BASE = 10000.0
MLA_HEADS = 4
MLA_NOPE = 128
MLA_ROPE = 64
MLA_V = 128
MLA_Q_RANK = 384
MLA_KV_RANK = 256
MLA_HEAD_PAD = 256
DIFF_HEADS = 4
DIFF_QK = 64
DIFF_V = 128
CHUNK = 128
GMLP_GROUPS = 4
GMLP_W = 512
POOL_WINDOWS = (2, 4, 8, 16)
POOL_PAD = 16
N_BRANCH = 4
BRANCH_W = 512
N_EXPERTS = 32
N_GROUPS = 4
EXPERTS_PER_GROUP = 8
D_EXPERT = 512
LN_EPS = 1e-5
RMS_EPS = 1e-6

SEG_DK, SEG_DV, SEG_DQ, SEG_PL, SEG_GM, SEG_CKV, SEG_KPE, SEG_CQ = 0, 512, 1024, 1536, 2048, 3072, 3328, 3456
N_SMALL = 3840


def _params(*sem):
    return pltpu.CompilerParams(dimension_semantics=sem, vmem_limit_bytes=VMEM_LIMIT)


def _const_spec(shape):
    nd = len(shape)
    return pl.BlockSpec(shape, lambda *_: (0,) * nd, pipeline_mode=pl.Buffered(1))


def _sigmoid(x):
    return 1.0 / (1.0 + jnp.exp(-x))


def _layernorm(x, g, b):
    mu = jnp.mean(x, axis=-1, keepdims=True)
    xc = x - mu
    var = jnp.mean(xc * xc, axis=-1, keepdims=True)
    return xc * lax.rsqrt(var + LN_EPS) * g + b


def _rmsnorm(x, g):
    return x * lax.rsqrt(jnp.mean(x * x, axis=-1, keepdims=True) + RMS_EPS) * g


def _rope(x, cos_t, sin_t):
    lane = lax.broadcasted_iota(I32, (x.shape[0], LANES), 1)
    first_half = (lane % ROPE_DIM) < (ROPE_DIM // 2)
    outs = []
    for j in range(x.shape[1] // LANES):
        xc = x[:, j * LANES:(j + 1) * LANES]
        partner = jnp.where(first_half, pltpu.roll(xc, LANES - ROPE_DIM // 2, 1), pltpu.roll(xc, ROPE_DIM // 2, 1))
        outs.append(xc * cos_t + partner * sin_t)
    return outs[0] if len(outs) == 1 else jnp.concatenate(outs, axis=1)


def _ada_kernel(c_ref, w_ref, b_ref, o_ref):
    c = c_ref[...]
    s = (c * _sigmoid(c)).astype(BF16)
    o_ref[0] = jnp.dot(s, w_ref[0].astype(BF16), preferred_element_type=F32) + b_ref[0]


def _ada(c_all, ada_w, ada_b):
    depth, d, n6 = ada_w.shape
    tn = 1024
    rows = c_all.shape[0]
    return pl.pallas_call(
        _ada_kernel,
        grid=(depth, n6 // tn),
        in_specs=[pl.BlockSpec((rows, d), lambda l, j: (0, 0)),
                  pl.BlockSpec((1, d, tn), lambda l, j: (l, 0, j)),
                  pl.BlockSpec((1, 1, tn), lambda l, j: (l, 0, j))],
        out_specs=pl.BlockSpec((1, rows, tn), lambda l, j: (l, 0, j)),
        out_shape=jax.ShapeDtypeStruct((depth, rows, n6), F32),
        compiler_params=_params("parallel", "parallel"),
        name="ada",
    )(c_all, ada_w, ada_b.reshape(depth, 1, n6))


def _inproj_kernel(x_ref, sc_ref, sh_ref, w_ref, o_ref):
    h = (x_ref[0] * (1.0 + sc_ref[0]) + sh_ref[0]).astype(BF16)
    o_ref[0] = jnp.dot(h, w_ref[...], preferred_element_type=F32).astype(o_ref.dtype)


def _mod_spec(n_ctx_tiles, d, order):
    if order == "jbq":
        return pl.BlockSpec((1, 1, d), lambda j, b, q: (2 * b + (q >= n_ctx_tiles).astype(I32), 0, 0))
    return pl.BlockSpec((1, 1, d), lambda b, q: (2 * b + (q >= n_ctx_tiles).astype(I32), 0, 0))


def _inproj(x, sc, sh, w, tn, n_ctx_tiles, name):
    bsz, l_all, d = x.shape
    n = w.shape[1]
    nt = l_all // TILE_M
    return pl.pallas_call(
        _inproj_kernel,
        grid=(n // tn, bsz, nt),
        in_specs=[pl.BlockSpec((1, TILE_M, d), lambda j, b, q: (b, q, 0)),
                  _mod_spec(n_ctx_tiles, d, "jbq"), _mod_spec(n_ctx_tiles, d, "jbq"),
                  pl.BlockSpec((d, tn), lambda j, b, q: (0, j))],
        out_specs=pl.BlockSpec((1, TILE_M, tn), lambda j, b, q: (b, q, j)),
        out_shape=jax.ShapeDtypeStruct((bsz, l_all, n), BF16),
        compiler_params=_params("parallel", "parallel", "parallel"),
        name=name,
    )(x, sc, sh, w)


def _prep_kernel(ckv_ref, kpe_ref, cq_ref, dq_ref, dk_ref, cos_ref, sin_ref, gkv_ref, gq_ref, wukv_ref, wuq_ref,
                 mq_ref, mk_ref, mv_ref, dqr_ref, dkr_ref, *, mla_scale, diff_scale):
    cos_t = cos_ref[...]
    sin_t = sin_ref[...]
    ckv = _rmsnorm(ckv_ref[0].astype(F32), gkv_ref[...]).astype(BF16)
    kv = jnp.dot(ckv, wukv_ref[...], preferred_element_type=F32)
    kpe = _rope(kpe_ref[0].astype(F32), cos_t, sin_t).astype(BF16)
    for h in range(MLA_HEADS):
        base = h * MLA_HEAD_PAD
        mk_ref[0, :, base:base + MLA_NOPE] = kv[:, h * MLA_NOPE:(h + 1) * MLA_NOPE].astype(BF16)
        mk_ref[0, :, base + MLA_NOPE:base + MLA_HEAD_PAD] = kpe
    mv_ref[0] = kv[:, MLA_HEADS * MLA_NOPE:].astype(BF16)
    cq = _rmsnorm(cq_ref[0].astype(F32), gq_ref[...]).astype(BF16)
    q = jnp.dot(cq, wuq_ref[...], preferred_element_type=F32)
    for h in range(MLA_HEADS):
        base = h * MLA_HEAD_PAD
        mq_ref[0, :, base:base + MLA_NOPE] = (q[:, base:base + MLA_NOPE] * mla_scale).astype(BF16)
        qpe = _rope(q[:, base + MLA_NOPE:base + MLA_HEAD_PAD], cos_t, sin_t)
        mq_ref[0, :, base + MLA_NOPE:base + MLA_HEAD_PAD] = (qpe * mla_scale).astype(BF16)
    dqr_ref[0] = (_rope(dq_ref[0].astype(F32), cos_t, sin_t) * diff_scale).astype(BF16)
    dkr_ref[0] = _rope(dk_ref[0].astype(F32), cos_t, sin_t).astype(BF16)


def _prep(p, cos_t, sin_t, g_kv, g_q, w_ukv, w_uq):
    bsz, l_all, _ = p.shape
    nt = l_all // TILE_M

    def seg(width, offset):
        return pl.BlockSpec((1, TILE_M, width), lambda b, q: (b, q, offset // width))

    def out(width):
        return pl.BlockSpec((1, TILE_M, width), lambda b, q: (b, q, 0))

    kern = functools.partial(_prep_kernel, mla_scale=(MLA_NOPE + MLA_ROPE) ** -0.5, diff_scale=DIFF_QK ** -0.5)
    widths = (MLA_HEADS * MLA_HEAD_PAD, MLA_HEADS * MLA_HEAD_PAD, MLA_HEADS * MLA_V, 512, 512)
    return pl.pallas_call(
        kern,
        grid=(bsz, nt),
        in_specs=[seg(MLA_KV_RANK, SEG_CKV), seg(LANES, SEG_KPE), seg(MLA_Q_RANK, SEG_CQ), seg(512, SEG_DQ),
                  seg(512, SEG_DK),
                  pl.BlockSpec((TILE_M, LANES), lambda b, q: (q, 0)), pl.BlockSpec((TILE_M, LANES), lambda b, q: (q, 0)),
                  _const_spec(g_kv.shape), _const_spec(g_q.shape), _const_spec(w_ukv.shape), _const_spec(w_uq.shape)],
        out_specs=[out(w) for w in widths],
        out_shape=[jax.ShapeDtypeStruct((bsz, l_all, w), BF16) for w in widths],
        compiler_params=_params("parallel", "parallel"),
        name="attn_prep",
    )(p, p, p, p, p, cos_t, sin_t, g_kv, g_q, w_ukv, w_uq)


def _softmax_parts(s):
    m = jnp.max(s, axis=-1, keepdims=True)
    p = jnp.exp(s - m)
    return p, jnp.sum(p, axis=-1, keepdims=True)


def _qk(q, k):
    return lax.dot_general(q, k, (((1,), (1,)), ((), ())), preferred_element_type=F32)


def _mla_kernel(q_ref, k_ref, v_ref, o_ref, *, n_ctx):
    q = q_ref[0]

    def run(nk):
        p, l = _softmax_parts(_qk(q, k_ref[0, :nk, :]))
        o = jnp.dot(p.astype(BF16), v_ref[0, :nk, :], preferred_element_type=F32) / l
        o_ref[0] = o.astype(o_ref.dtype)

    is_ctx = pl.program_id(2) < n_ctx // TILE_M
    pl.when(is_ctx)(lambda: run(n_ctx))
    pl.when(jnp.logical_not(is_ctx))(lambda: run(k_ref.shape[1]))


def _mla_attention(mq, mk, mv, n_ctx):
    bsz, l_all, _ = mq.shape
    nt = l_all // TILE_M
    return pl.pallas_call(
        functools.partial(_mla_kernel, n_ctx=n_ctx),
        grid=(bsz, MLA_HEADS, nt),
        in_specs=[pl.BlockSpec((1, TILE_M, MLA_HEAD_PAD), lambda b, h, q: (b, q, h)),
                  pl.BlockSpec((1, l_all, MLA_HEAD_PAD), lambda b, h, q: (b, 0, h)),
                  pl.BlockSpec((1, l_all, MLA_V), lambda b, h, q: (b, 0, h))],
        out_specs=pl.BlockSpec((1, TILE_M, MLA_V), lambda b, h, q: (b, q, h)),
        out_shape=jax.ShapeDtypeStruct((bsz, l_all, MLA_HEADS * MLA_V), BF16),
        compiler_params=_params("parallel", "parallel", "parallel"),
        name="mla_attn",
    )(mq, mk, mv)


def _diff_kernel(q_ref, k_ref, v_ref, lam_ref, g_ref, o_ref, *, n_ctx, lam_init):
    lv = lam_ref[...]
    lam = (jnp.exp(jnp.sum(lv[0:1] * lv[1:2], axis=-1, keepdims=True))
           - jnp.exp(jnp.sum(lv[2:3] * lv[3:4], axis=-1, keepdims=True)) + lam_init)
    q = q_ref[0]
    lane = lax.broadcasted_iota(I32, q.shape, 1)
    q1 = jnp.where(lane < DIFF_QK, q, jnp.zeros_like(q))
    q2 = jnp.where(lane >= DIFF_QK, q, jnp.zeros_like(q))

    def run(nk):
        k = k_ref[0, :nk, :]
        p1, l1 = _softmax_parts(_qk(q1, k))
        p2, l2 = _softmax_parts(_qk(q2, k))
        w = p1 * (1.0 / l1) - p2 * (lam / l2)
        o = jnp.dot(w.astype(BF16), v_ref[0, :nk, :], preferred_element_type=F32)
        o_ref[0] = (_rmsnorm(o, g_ref[...]) * (1.0 - lam_init)).astype(o_ref.dtype)

    is_ctx = pl.program_id(2) < n_ctx // TILE_M
    pl.when(is_ctx)(lambda: run(n_ctx))
    pl.when(jnp.logical_not(is_ctx))(lambda: run(k_ref.shape[1]))


def _diff_attention(dqr, dkr, p, lam_vec, g_sub, n_ctx, lam_init):
    bsz, l_all, _ = dqr.shape
    nt = l_all // TILE_M
    w = 2 * DIFF_QK
    return pl.pallas_call(
        functools.partial(_diff_kernel, n_ctx=n_ctx, lam_init=lam_init),
        grid=(bsz, DIFF_HEADS, nt),
        in_specs=[pl.BlockSpec((1, TILE_M, w), lambda b, h, q: (b, q, h)),
                  pl.BlockSpec((1, l_all, w), lambda b, h, q: (b, 0, h)),
                  pl.BlockSpec((1, l_all, DIFF_V), lambda b, h, q: (b, 0, SEG_DV // DIFF_V + h)),
                  _const_spec(lam_vec.shape), _const_spec(g_sub.shape)],
        out_specs=pl.BlockSpec((1, TILE_M, DIFF_V), lambda b, h, q: (b, q, h)),
        out_shape=jax.ShapeDtypeStruct((bsz, l_all, DIFF_HEADS * DIFF_V), BF16),
        compiler_params=_params("parallel", "parallel", "parallel"),
        name="diff_attn",
    )(dqr, dkr, p, lam_vec, g_sub)


def _gelu_tanh(x):
    return 0.5 * x * (1.0 + jnp.tanh(math.sqrt(2.0 / math.pi) * (x + 0.044715 * (x * x * x))))


def _gmlp_kernel(z_ref, g_ref, b_ref, ws_ref, bst_ref, o_ref):
    z = _gelu_tanh(z_ref[0].astype(F32))
    u = z[:, :GMLP_W]
    v = _layernorm(z[:, GMLP_W:], g_ref[...], b_ref[...]).astype(BF16)
    gw = GMLP_W // GMLP_GROUPS
    for c in range(z.shape[0] // CHUNK):
        rows = slice(c * CHUNK, (c + 1) * CHUNK)
        for g in range(GMLP_GROUPS):
            cols = slice(g * gw, (g + 1) * gw)
            s = jnp.dot(ws_ref[g], v[rows, cols], preferred_element_type=F32) + bst_ref[:, g:g + 1]
            o_ref[0, rows, cols] = (u[rows, cols] * s).astype(o_ref.dtype)


def _gmlp(p, ln_g, ln_b, ws, bs_t):
    bsz, l_all, _ = p.shape
    nt = l_all // TILE_M
    return pl.pallas_call(
        _gmlp_kernel,
        grid=(bsz, nt),
        in_specs=[pl.BlockSpec((1, TILE_M, 2 * GMLP_W), lambda b, q: (b, q, SEG_GM // (2 * GMLP_W))),
                  _const_spec(ln_g.shape), _const_spec(ln_b.shape), _const_spec(ws.shape), _const_spec(bs_t.shape)],
        out_specs=pl.BlockSpec((1, TILE_M, GMLP_W), lambda b, q: (b, q, 0)),
        out_shape=jax.ShapeDtypeStruct((bsz, l_all, GMLP_W), BF16),
        compiler_params=_params("parallel", "parallel"),
        name="gmlp",
    )(p, ln_g, ln_b, ws, bs_t)


def _pool_kernel(p_ref, w_ref, sc_ref, o_ref, pad_ref, *, n_ctx):
    n_all = p_ref.shape[1]
    gw = LANES
    t = lax.broadcasted_iota(I32, (n_all, gw), 0)
    seq_lo = jnp.where(t < n_ctx, 0, n_ctx)
    seq_hi = jnp.where(t < n_ctx, n_ctx, n_all)
    pad_ref[0:POOL_PAD, :] = jnp.zeros((POOL_PAD, gw), F32)
    pad_ref[POOL_PAD + n_all:, :] = jnp.zeros((POOL_PAD, gw), F32)
    for i, win in enumerate(POOL_WINDOWS):
        cols = slice(i * gw, (i + 1) * gw)
        x = p_ref[0, :, cols].astype(F32)
        pad_ref[POOL_PAD:POOL_PAD + n_all, :] = x
        acc = jnp.zeros((n_all, gw), F32)
        cnt = jnp.zeros((n_all, gw), F32)
        for d in range(-(win // 2), win - win // 2):
            valid = jnp.logical_and(t + d >= seq_lo, t + d < seq_hi)
            acc = acc + jnp.where(valid, pad_ref[POOL_PAD + d:POOL_PAD + d + n_all, :], 0.0)
            cnt = cnt + valid.astype(F32)
        resid = (acc / cnt - x).astype(BF16)
        o = jnp.dot(resid, w_ref[i], preferred_element_type=F32) * sc_ref[:, cols]
        o_ref[0, :, cols] = o.astype(o_ref.dtype)


def _pool(p, pool_w, pool_scale, n_ctx):
    bsz, l_all, _ = p.shape
    width = len(POOL_WINDOWS) * LANES
    return pl.pallas_call(
        functools.partial(_pool_kernel, n_ctx=n_ctx),
        grid=(bsz,),
        in_specs=[pl.BlockSpec((1, l_all, width), lambda b: (b, 0, SEG_PL // width)),
                  _const_spec(pool_w.shape), _const_spec(pool_scale.shape)],
        out_specs=pl.BlockSpec((1, l_all, width), lambda b: (b, 0, 0)),
        out_shape=jax.ShapeDtypeStruct((bsz, l_all, width), BF16),
        scratch_shapes=[pltpu.VMEM((l_all + 2 * POOL_PAD, LANES), F32)],
        compiler_params=_params("parallel"),
        name="pool",
    )(p, pool_w, pool_scale)


def _merge_kernel(b0_ref, b1_ref, b2_ref, b3_ref, gl_ref, x_ref, g1_ref, wb_ref, wo_ref, lg_ref, lb_ref, o_ref, *,
                  alpha):
    d = x_ref.shape[2]
    merged = None
    for n, br in enumerate((b0_ref, b1_ref, b2_ref, b3_ref)):
        t = jnp.dot(br[0], wb_ref[n], preferred_element_type=F32)
        term = _sigmoid(gl_ref[0, :, n * d:(n + 1) * d].astype(F32)) * t
        merged = term if merged is None else merged + term
    y = jnp.dot(merged.astype(BF16), wo_ref[...], preferred_element_type=F32)
    o_ref[0] = _layernorm(alpha * x_ref[0] + g1_ref[0] * y, lg_ref[...], lb_ref[...])


def _merge(branches, gl, x, g1, w_branch, w_out, ln_g, ln_b, n_ctx_tiles, alpha):
    bsz, l_all, d = x.shape
    nt = l_all // TILE_M
    row = lambda width: pl.BlockSpec((1, TILE_M, width), lambda b, q: (b, q, 0))
    return pl.pallas_call(
        functools.partial(_merge_kernel, alpha=alpha),
        grid=(bsz, nt),
        in_specs=[row(BRANCH_W)] * N_BRANCH + [row(N_BRANCH * d), row(d), _mod_spec(n_ctx_tiles, d, "bq"),
                                               _const_spec(w_branch.shape), _const_spec(w_out.shape),
                                               _const_spec(ln_g.shape), _const_spec(ln_b.shape)],
        out_specs=row(d),
        out_shape=jax.ShapeDtypeStruct((bsz, l_all, d), F32),
        compiler_params=_params("parallel", "parallel"),
        name="merge",
    )(*branches, gl, x, g1, w_branch, w_out, ln_g, ln_b)


def _pack_rows(ref, y):
    m, d = y.shape
    bits = pltpu.bitcast(y.astype(BF16).astype(F32), U32)
    words = (bits[:, :d // 2] >> 16) | bits[:, d // 2:]
    for s in range(SUBLANES):
        ref[pl.ds(s, m, stride=SUBLANES), :] = words[:, s * LANES:(s + 1) * LANES]


def _unpack_rows(ref, m, first=0):
    lo, hi = [], []
    for s in range(SUBLANES):
        w = ref[pl.ds(first + s, m, stride=SUBLANES), :]
        lo.append(pltpu.bitcast(w << 16, F32))
        hi.append(pltpu.bitcast(w & jnp.uint32(0xFFFF0000), F32))
    return lo, hi


def _route_kernel(x_ref, sc_ref, sh_ref, rw_ref, rb_ref, hp_ref, idx_ref, wt_ref):
    h = x_ref[0] * (1.0 + sc_ref[0]) + sh_ref[0]
    logits = jnp.dot(h, rw_ref[...], precision=lax.Precision.HIGHEST, preferred_element_type=F32)
    scores = _sigmoid(logits)
    biased = scores + rb_ref[...]
    lane = lax.broadcasted_iota(I32, biased.shape, 1)
    neg = jnp.float32(-jnp.inf)
    best = None
    for g in range(N_GROUPS):
        in_g = jnp.logical_and(lane >= g * EXPERTS_PER_GROUP, lane < (g + 1) * EXPERTS_PER_GROUP)
        a = jnp.where(in_g, biased, neg)
        m1 = jnp.max(a, axis=-1, keepdims=True)
        i1 = jnp.min(jnp.where(a == m1, lane, LANES), axis=-1, keepdims=True)
        a2 = jnp.where(lane == i1, neg, a)
        m2 = jnp.max(a2, axis=-1, keepdims=True)
        i2 = jnp.min(jnp.where(a2 == m2, lane, LANES), axis=-1, keepdims=True)
        gs = m1 + m2
        if best is None:
            best = (gs, i1, i2)
        else:
            take = gs > best[0]
            best = (jnp.where(take, gs, best[0]), jnp.where(take, i1, best[1]), jnp.where(take, i2, best[2]))
    _, e0, e1 = best
    w0 = jnp.sum(jnp.where(lane == e0, scores, 0.0), axis=-1, keepdims=True)
    w1 = jnp.sum(jnp.where(lane == e1, scores, 0.0), axis=-1, keepdims=True)
    tot = w0 + w1
    idx_ref[...] = jnp.where(lane == 0, e0, jnp.where(lane == 1, e1, 0))
    wt_ref[...] = jnp.where(lane == 0, w0 / tot, jnp.where(lane == 1, w1 / tot, 0.0))
    _pack_rows(hp_ref, h)


def _route(x, sc, sh, rw, rb, n_ctx_tiles):
    bsz, l_all, d = x.shape
    nt = l_all // TILE_M
    t_all = bsz * l_all
    flat = lambda b, q: (b * nt + q, 0)
    return pl.pallas_call(
        _route_kernel,
        grid=(bsz, nt),
        in_specs=[pl.BlockSpec((1, TILE_M, d), lambda b, q: (b, q, 0)),
                  _mod_spec(n_ctx_tiles, d, "bq"), _mod_spec(n_ctx_tiles, d, "bq"),
                  _const_spec(rw.shape), _const_spec(rb.shape)],
        out_specs=[pl.BlockSpec((TILE_M * SUBLANES, LANES), flat), pl.BlockSpec((TILE_M, LANES), flat),
                   pl.BlockSpec((TILE_M, LANES), flat)],
        out_shape=[jax.ShapeDtypeStruct((t_all * SUBLANES, LANES), U32), jax.ShapeDtypeStruct((t_all, LANES), I32),
                   jax.ShapeDtypeStruct((t_all, LANES), F32)],
        compiler_params=_params("parallel", "parallel"),
        name="moe_route",
    )(x, sc, sh, rw, rb)


def _route_tables(e_pairs, t_all, n_tiles):
    onehot = (e_pairs[:, None] == jnp.arange(N_EXPERTS, dtype=I32)[None, :]).astype(I32)
    csum = jnp.cumsum(onehot, axis=0)
    rank = jnp.sum((csum - onehot) * onehot, axis=1)
    counts = csum[-1]
    padded = ((counts + TILE_M - 1) // TILE_M) * TILE_M
    ends = jnp.cumsum(padded)
    pos = (ends - padded)[e_pairs] + rank
    tok = jnp.tile(jnp.arange(t_all, dtype=I32), 2)
    src_tok = jnp.zeros((n_tiles * TILE_M,), I32).at[pos].set(tok)
    tile_start = jnp.arange(n_tiles, dtype=I32) * TILE_M
    tile_expert = jnp.minimum(jnp.sum((tile_start[:, None] >= ends[None, :]).astype(I32), axis=1), N_EXPERTS - 1)
    tile_valid = (tile_start < ends[-1]).astype(I32)
    return src_tok, pos.astype(I32), tile_expert, tile_valid


def _start_row_gather(idx_ref, base, src_ref, dst_ref, sem, n_rows):
    def body(r, carry):
        src_at = pl.multiple_of(idx_ref[base + r] * SUBLANES, SUBLANES)
        dst_at = pl.multiple_of(r * SUBLANES, SUBLANES)
        pltpu.make_async_copy(src_ref.at[pl.ds(src_at, SUBLANES)], dst_ref.at[pl.ds(dst_at, SUBLANES)], sem).start()
        return carry

    lax.fori_loop(0, n_rows, body, 0, unroll=8)


def _wait_row_gather(src_ref, dst_ref, sem):
    pltpu.make_async_copy(src_ref.at[pl.ds(0, dst_ref.shape[0])], dst_ref, sem).wait()


def _expert_kernel(te_ref, tv_ref, tok_ref, hp_ref, w1_ref, w3_ref, w2_ref, o_ref, rows_ref, xb_ref, sem):
    del te_ref
    i = pl.program_id(0)
    slot = i % 2

    def fetch(tile, s):
        _start_row_gather(tok_ref, tile * TILE_M, hp_ref, rows_ref.at[s], sem.at[s], TILE_M)

    pl.when(i == 0)(lambda: fetch(0, 0))
    pl.when(i + 1 < pl.num_programs(0))(lambda: fetch(i + 1, 1 - slot))
    _wait_row_gather(hp_ref, rows_ref.at[slot], sem.at[slot])
    valid = tv_ref[i] == 1

    @pl.when(valid)
    def _():
        half = xb_ref.shape[1] // 2
        lo, hi = _unpack_rows(rows_ref.at[slot], TILE_M)
        for s in range(SUBLANES):
            xb_ref[:, s * LANES:(s + 1) * LANES] = lo[s].astype(BF16)
            xb_ref[:, half + s * LANES:half + (s + 1) * LANES] = hi[s].astype(BF16)
        x = xb_ref[...]
        a = jnp.dot(x, w1_ref[0], preferred_element_type=F32)
        b = jnp.dot(x, w3_ref[0], preferred_element_type=F32)
        hid = (a * _sigmoid(a) * b).astype(BF16)
        _pack_rows(o_ref, jnp.dot(hid, w2_ref[0], preferred_element_type=F32))

    @pl.when(jnp.logical_not(valid))
    def _():
        o_ref[...] = jnp.zeros(o_ref.shape, o_ref.dtype)


def _experts(hp, src_tok, tile_expert, tile_valid, w1, w3, w2):
    n_tiles = tile_expert.shape[0]
    _, d, de = w1.shape
    blk = TILE_M * SUBLANES
    return pl.pallas_call(
        _expert_kernel,
        grid_spec=pltpu.PrefetchScalarGridSpec(
            num_scalar_prefetch=3,
            grid=(n_tiles,),
            in_specs=[pl.BlockSpec(memory_space=pl.ANY),
                      pl.BlockSpec((1, d, de), lambda i, te, tv, tok: (te[i], 0, 0)),
                      pl.BlockSpec((1, d, de), lambda i, te, tv, tok: (te[i], 0, 0)),
                      pl.BlockSpec((1, de, d), lambda i, te, tv, tok: (te[i], 0, 0))],
            out_specs=pl.BlockSpec((blk, LANES), lambda i, te, tv, tok: (i, 0)),
            scratch_shapes=[pltpu.VMEM((2, blk, LANES), U32), pltpu.VMEM((TILE_M, d), BF16),
                            pltpu.SemaphoreType.DMA((2,))]),
        out_shape=jax.ShapeDtypeStruct((n_tiles * blk, LANES), U32),
        compiler_params=_params("arbitrary"),
        name="moe_experts",
    )(tile_expert, tile_valid, src_tok, hp, w1, w3, w2)


def _combine_kernel(pos_ref, ys_ref, wt_ref, x_ref, g2_ref, lg_ref, lb_ref, o_ref, rows_ref, sem, *, alpha, t_all):
    i = pl.program_id(0)
    slot = i % 2
    blk = TILE_M * SUBLANES

    def fetch(tile, s):
        for k in range(2):
            _start_row_gather(pos_ref, k * t_all + tile * TILE_M, ys_ref, rows_ref.at[s, pl.ds(k * blk, blk)],
                              sem.at[s], TILE_M)

    pl.when(i == 0)(lambda: fetch(0, 0))
    pl.when(i + 1 < pl.num_programs(0))(lambda: fetch(i + 1, 1 - slot))
    _wait_row_gather(ys_ref, rows_ref.at[slot], sem.at[slot])
    w = wt_ref[...]
    lo0, hi0 = _unpack_rows(rows_ref.at[slot], TILE_M)
    lo1, hi1 = _unpack_rows(rows_ref.at[slot], TILE_M, blk)
    y = w[:, 0:1] * jnp.concatenate(lo0 + hi0, axis=1) + w[:, 1:2] * jnp.concatenate(lo1 + hi1, axis=1)
    o_ref[0] = _layernorm(alpha * x_ref[0] + g2_ref[0] * y, lg_ref[...], lb_ref[...])


def _combine(ys, pos, wt, x, g2, ln_g, ln_b, n_ctx_tiles, alpha):
    bsz, l_all, d = x.shape
    nt = l_all // TILE_M
    blk = TILE_M * SUBLANES
    return pl.pallas_call(
        functools.partial(_combine_kernel, alpha=alpha, t_all=bsz * l_all),
        grid_spec=pltpu.PrefetchScalarGridSpec(
            num_scalar_prefetch=1,
            grid=(bsz * nt,),
            in_specs=[pl.BlockSpec(memory_space=pl.ANY),
                      pl.BlockSpec((TILE_M, LANES), lambda i, pos: (i, 0)),
                      pl.BlockSpec((1, TILE_M, d), lambda i, pos: (i // nt, i % nt, 0)),
                      pl.BlockSpec((1, 1, d), lambda i, pos: (2 * (i // nt) + (i % nt >= n_ctx_tiles).astype(I32), 0, 0)),
                      pl.BlockSpec(ln_g.shape, lambda i, pos: (0, 0), pipeline_mode=pl.Buffered(1)),
                      pl.BlockSpec(ln_b.shape, lambda i, pos: (0, 0), pipeline_mode=pl.Buffered(1))],
            out_specs=pl.BlockSpec((1, TILE_M, d), lambda i, pos: (i // nt, i % nt, 0)),
            scratch_shapes=[pltpu.VMEM((2, 2 * blk, LANES), U32), pltpu.SemaphoreType.DMA((2,))]),
        out_shape=jax.ShapeDtypeStruct((bsz, l_all, d), F32),
        compiler_params=_params("arbitrary"),
        name="moe_combine",
    )(pos, ys, wt, x, g2, ln_g, ln_b)


def _rope_tables(n_ctx, n_lat):
    rows = n_lat // GRID_W
    row = jnp.repeat(jnp.arange(rows, dtype=F32), GRID_W)
    col = jnp.tile(jnp.arange(GRID_W, dtype=F32), rows)
    n_freq = ROPE_DIM // 4
    inv = ROPE_BASE ** (-jnp.arange(n_freq, dtype=F32) / n_freq)
    ang = jnp.concatenate([row[:, None] * inv, col[:, None] * inv], axis=-1)
    cos = jnp.concatenate([jnp.ones((n_ctx, ROPE_DIM // 2), F32), jnp.cos(ang)], axis=0)
    sin = jnp.concatenate([jnp.zeros((n_ctx, ROPE_DIM // 2), F32), jnp.sin(ang)], axis=0)
    reps = LANES // ROPE_DIM
    return jnp.tile(jnp.concatenate([cos, cos], axis=1), (1, reps)), jnp.tile(jnp.concatenate([-sin, sin], axis=1), (1, reps))


def _split_w_in(w):
    d = w.shape[0]
    o = 0
    segs = {}
    for name, width in (("ckv", MLA_KV_RANK), ("kpe", MLA_ROPE), ("dk", 512), ("dv", 512), ("cq", MLA_Q_RANK),
                        ("dq", 512), ("gm", 2 * GMLP_W), ("pl", 512), ("gate", N_BRANCH * d)):
        segs[name] = w[:, o:o + width]
        o += width
    small = jnp.concatenate([segs["dk"], segs["dv"], segs["dq"], segs["pl"], segs["gm"], segs["ckv"], segs["kpe"],
                             jnp.zeros((d, LANES - MLA_ROPE), w.dtype), segs["cq"]], axis=1)
    return small.astype(BF16), segs["gate"].astype(BF16)


def _forward(x, c, ctx, c_ctx, ada_w, ada_b, w_in, mla_q_norm, mla_kv_norm, mla_w_uq, mla_w_ukv, diff_lambda,
             diff_subln, gmlp_ln_g, gmlp_ln_b, gmlp_ws, gmlp_bs, pool_w, pool_scale, w_branch, w_out, ln1_g, ln1_b,
             ln2_g, ln2_b, router_w, router_bias, moe_w1, moe_w3, moe_w2):
    bsz, n_lat, d = x.shape
    n_ctx = ctx.shape[1]
    depth = w_in.shape[0]
    l_all = n_ctx + n_lat
    t_all = bsz * l_all
    n_ctx_tiles = n_ctx // TILE_M
    alpha = (2 * depth) ** 0.25
    n_row_tiles = (2 * t_all) // TILE_M + N_EXPERTS

    xs = jnp.concatenate([ctx, x], axis=1)
    cos_t, sin_t = _rope_tables(n_ctx, n_lat)

    c_rows = ((bsz + 1 + SUBLANES - 1) // SUBLANES) * SUBLANES
    c_all = jnp.zeros((c_rows, d), F32).at[:bsz].set(c).at[bsz].set(c_ctx)
    mod = _ada(c_all, ada_w, ada_b)
    pick = jnp.stack([jnp.full((bsz,), bsz, I32), jnp.arange(bsz, dtype=I32)], axis=1).reshape(-1)
    mod = mod[:, pick].reshape(depth, 2 * bsz, 1, 6, d)

    rw = jnp.zeros((d, LANES), F32).at[:, :N_EXPERTS].set(router_w)
    rb = jnp.zeros((1, LANES), F32).at[0, :N_EXPERTS].set(router_bias)

    for l in range(depth):
        sh1, sc1, g1, sh2, sc2, g2 = (mod[l, :, :, k] for k in range(6))
        lam_init = 0.8 - 0.6 * math.exp(-0.3 * l)
        w_small, w_gate = _split_w_in(w_in[l])
        w_ukv = mla_w_ukv[l].reshape(MLA_KV_RANK, MLA_HEADS, 2, MLA_NOPE).transpose(0, 2, 1, 3).reshape(MLA_KV_RANK, -1)
        w_uq = jnp.pad(mla_w_uq[l].reshape(MLA_Q_RANK, MLA_HEADS, MLA_NOPE + MLA_ROPE),
                       ((0, 0), (0, 0), (0, MLA_HEAD_PAD - MLA_NOPE - MLA_ROPE))).reshape(MLA_Q_RANK, -1)

        p = _inproj(xs, sc1, sh1, w_small, 1280, n_ctx_tiles, "inproj_small")
        gl = _inproj(xs, sc1, sh1, w_gate, 2048, n_ctx_tiles, "inproj_gate")
        mq, mk, mv, dqr, dkr = _prep(p, cos_t, sin_t, mla_kv_norm[l][None], mla_q_norm[l][None],
                                     w_ukv.astype(BF16), w_uq.astype(BF16))
        mla_out = _mla_attention(mq, mk, mv, n_ctx)
        d_out = _diff_attention(dqr, dkr, p, diff_lambda[l], diff_subln[l][None], n_ctx, lam_init)
        g_out = _gmlp(p, gmlp_ln_g[l][None], gmlp_ln_b[l][None], gmlp_ws[l].astype(BF16), gmlp_bs[l].T)
        p_out = _pool(p, pool_w[l].astype(BF16), pool_scale[l][None], n_ctx)
        xs = _merge((mla_out, d_out, g_out, p_out), gl, xs, g1, w_branch[l].astype(BF16), w_out[l].astype(BF16),
                    ln1_g[l][None], ln1_b[l][None], n_ctx_tiles, alpha)

        hp, idx, wt = _route(xs, sc2, sh2, rw, rb, n_ctx_tiles)
        e_pairs = idx[:, :2].T.reshape(-1)
        src_tok, pos, tile_expert, tile_valid = _route_tables(e_pairs, t_all, n_row_tiles)
        ys = _experts(hp, src_tok, tile_expert, tile_valid, moe_w1[l].astype(BF16), moe_w3[l].astype(BF16),
                      moe_w2[l].astype(BF16))
        xs = _combine(ys, pos, wt, xs, g2, ln2_g[l][None], ln2_b[l][None], n_ctx_tiles, alpha)
    return xs[:, n_ctx:]


_forward_jit = jax.jit(_forward)


def kernel(x, c, ctx, c_ctx, ada_w, ada_b, w_in, mla_q_norm, mla_kv_norm, mla_w_uq, mla_w_ukv, diff_lambda, diff_subln, gmlp_ln_g, gmlp_ln_b, gmlp_ws, gmlp_bs, pool_w, pool_scale, w_branch, w_out, ln1_g, ln1_b, ln2_g, ln2_b, router_w, router_bias, moe_w1, moe_w3, moe_w2):
    return _forward_jit(x, c, ctx, c_ctx, ada_w, ada_b, w_in, mla_q_norm, mla_kv_norm, mla_w_uq, mla_w_ukv,
                        diff_lambda, diff_subln, gmlp_ln_g, gmlp_ln_b, gmlp_ws, gmlp_bs, pool_w, pool_scale,
                        w_branch, w_out, ln1_g, ln1_b, ln2_g, ln2_b, router_w, router_bias, moe_w1, moe_w3, moe_w2)
```

```python
import functools
import math

import jax
import jax.numpy as jnp
from jax import lax
from jax.experimental import pallas as pl
from jax.experimental.pallas import tpu as pltpu

F32 = jnp.float32
BF16 = jnp.bfloat16
U32 = jnp.uint32
I32 = jnp.int32

LANES = 128
SUBLANES = 8
TILE_M = 256
VMEM_LIMIT = 56 << 20

GRID_W = 64
ROPE_DIM = 64
ROPE_BASE = 10000.0
MLA_HEADS = 4
MLA_NOPE = 128
MLA_ROPE = 64
MLA_V = 128
MLA_Q_RANK = 384
MLA_KV_RANK = 256
MLA_HEAD_PAD = 256
DIFF_HEADS = 4
DIFF_QK = 64
DIFF_V = 128
CHUNK = 128
GMLP_GROUPS = 4
GMLP_W = 512
POOL_WINDOWS = (2, 4, 8, 16)
POOL_PAD = 16
N_BRANCH = 4
BRANCH_W = 512
N_EXPERTS = 32
N_GROUPS = 4
EXPERTS_PER_GROUP = 8
D_EXPERT = 512
LN_EPS = 1e-5
RMS_EPS = 1e-6

SEG_DK, SEG_DV, SEG_DQ, SEG_PL, SEG_GM, SEG_CKV, SEG_KPE, SEG_CQ = 0, 512, 1024, 1536, 2048, 3072, 3328, 3456
N_SMALL = 3840


def _params(*sem):
    return pltpu.CompilerParams(dimension_semantics=sem, vmem_limit_bytes=VMEM_LIMIT)


def _const_spec(shape):
    nd = len(shape)
    return pl.BlockSpec(shape, lambda *_: (0,) * nd, pipeline_mode=pl.Buffered(1))


def _sigmoid(x):
    return 1.0 / (1.0 + jnp.exp(-x))


def _sigmoid_t(x):
    return 0.5 * jnp.tanh(0.5 * x) + 0.5


def _layernorm(x, g, b):
    mu = jnp.mean(x, axis=-1, keepdims=True)
    xc = x - mu
    var = jnp.mean(xc * xc, axis=-1, keepdims=True)
    return xc * lax.rsqrt(var + LN_EPS) * g + b


def _rmsnorm(x, g):
    return x * lax.rsqrt(jnp.mean(x * x, axis=-1, keepdims=True) + RMS_EPS) * g


def _rope(x, cos_t, sin_t):
    lane = lax.broadcasted_iota(I32, (x.shape[0], LANES), 1)
    first_half = (lane % ROPE_DIM) < (ROPE_DIM // 2)
    outs = []
    for j in range(x.shape[1] // LANES):
        xc = x[:, j * LANES:(j + 1) * LANES]
        partner = jnp.where(first_half, pltpu.roll(xc, LANES - ROPE_DIM // 2, 1), pltpu.roll(xc, ROPE_DIM // 2, 1))
        outs.append(xc * cos_t + partner * sin_t)
    return outs[0] if len(outs) == 1 else jnp.concatenate(outs, axis=1)


def _ada_kernel(c_ref, w_ref, b_ref, o_ref):
    c = c_ref[...]
    s = (c * _sigmoid(c)).astype(BF16)
    o_ref[0] = jnp.dot(s, w_ref[0].astype(BF16), preferred_element_type=F32) + b_ref[0]


def _ada(c_all, ada_w, ada_b):
    depth, d, n6 = ada_w.shape
    tn = 1024
    rows = c_all.shape[0]
    return pl.pallas_call(
        _ada_kernel,
        grid=(depth, n6 // tn),
        in_specs=[pl.BlockSpec((rows, d), lambda l, j: (0, 0)),
                  pl.BlockSpec((1, d, tn), lambda l, j: (l, 0, j)),
                  pl.BlockSpec((1, 1, tn), lambda l, j: (l, 0, j))],
        out_specs=pl.BlockSpec((1, rows, tn), lambda l, j: (l, 0, j)),
        out_shape=jax.ShapeDtypeStruct((depth, rows, n6), F32),
        compiler_params=_params("parallel", "parallel"),
        name="ada",
    )(c_all, ada_w, ada_b.reshape(depth, 1, n6))


def _inproj_kernel(x_ref, sc_ref, sh_ref, w_ref, o_ref):
    h = (x_ref[0] * (1.0 + sc_ref[0]) + sh_ref[0]).astype(BF16)
    o_ref[0] = jnp.dot(h, w_ref[...], preferred_element_type=F32).astype(o_ref.dtype)


def _mod_spec(n_ctx_tiles, d, order):
    if order == "jbq":
        return pl.BlockSpec((1, 1, d), lambda j, b, q: (2 * b + (q >= n_ctx_tiles).astype(I32), 0, 0))
    return pl.BlockSpec((1, 1, d), lambda b, q: (2 * b + (q >= n_ctx_tiles).astype(I32), 0, 0))


def _inproj(x, sc, sh, w, tn, n_ctx_tiles, name):
    bsz, l_all, d = x.shape
    n = w.shape[1]
    nt = l_all // TILE_M
    return pl.pallas_call(
        _inproj_kernel,
        grid=(n // tn, bsz, nt),
        in_specs=[pl.BlockSpec((1, TILE_M, d), lambda j, b, q: (b, q, 0)),
                  _mod_spec(n_ctx_tiles, d, "jbq"), _mod_spec(n_ctx_tiles, d, "jbq"),
                  pl.BlockSpec((d, tn), lambda j, b, q: (0, j))],
        out_specs=pl.BlockSpec((1, TILE_M, tn), lambda j, b, q: (b, q, j)),
        out_shape=jax.ShapeDtypeStruct((bsz, l_all, n), BF16),
        compiler_params=_params("parallel", "parallel", "parallel"),
        name=name,
    )(x, sc, sh, w)


def _prep_kernel(ckv_ref, kpe_ref, cq_ref, dq_ref, dk_ref, dv_ref, cos_ref, sin_ref, gkv_ref, gq_ref, wukv_ref, wuq_ref,
                 mq_ref, mk_ref, mv_ref, dqr_ref, dkr_ref, dvx_ref, *, mla_scale, diff_scale):
    cos_t = cos_ref[...]
    sin_t = sin_ref[...]
    ckv = _rmsnorm(ckv_ref[0].astype(F32), gkv_ref[...]).astype(BF16)
    kv = jnp.dot(ckv, wukv_ref[...], preferred_element_type=F32)
    kpe = _rope(kpe_ref[0].astype(F32), cos_t, sin_t).astype(BF16)
    lane = lax.broadcasted_iota(I32, (kv.shape[0], LANES), 1)
    ones_col = jnp.where(lane == 0, 1.0, 0.0).astype(BF16)
    for h in range(MLA_HEADS):
        base = h * MLA_HEAD_PAD
        mk_ref[0, :, base:base + MLA_NOPE] = kv[:, h * MLA_NOPE:(h + 1) * MLA_NOPE].astype(BF16)
        mk_ref[0, :, base + MLA_NOPE:base + MLA_HEAD_PAD] = kpe
        vcol = MLA_HEADS * MLA_NOPE + h * MLA_V
        mv_ref[0, :, 2 * h * LANES:(2 * h + 1) * LANES] = kv[:, vcol:vcol + MLA_V].astype(BF16)
        mv_ref[0, :, (2 * h + 1) * LANES:(2 * h + 2) * LANES] = ones_col
        dvx_ref[0, :, 2 * h * LANES:(2 * h + 1) * LANES] = dv_ref[0, :, h * DIFF_V:(h + 1) * DIFF_V]
        dvx_ref[0, :, (2 * h + 1) * LANES:(2 * h + 2) * LANES] = ones_col
    cq = _rmsnorm(cq_ref[0].astype(F32), gq_ref[...]).astype(BF16)
    q = jnp.dot(cq, wuq_ref[...], preferred_element_type=F32)
    for h in range(MLA_HEADS):
        base = h * MLA_HEAD_PAD
        mq_ref[0, :, base:base + MLA_NOPE] = (q[:, base:base + MLA_NOPE] * mla_scale).astype(BF16)
        qpe = _rope(q[:, base + MLA_NOPE:base + MLA_HEAD_PAD], cos_t, sin_t)
        mq_ref[0, :, base + MLA_NOPE:base + MLA_HEAD_PAD] = (qpe * mla_scale).astype(BF16)
    dqr_ref[0] = (_rope(dq_ref[0].astype(F32), cos_t, sin_t) * diff_scale).astype(BF16)
    dkr_ref[0] = _rope(dk_ref[0].astype(F32), cos_t, sin_t).astype(BF16)


def _prep(p, cos_t, sin_t, g_kv, g_q, w_ukv, w_uq):
    bsz, l_all, _ = p.shape
    nt = l_all // TILE_M

    def seg(width, offset):
        return pl.BlockSpec((1, TILE_M, width), lambda b, q: (b, q, offset // width))

    def out(width):
        return pl.BlockSpec((1, TILE_M, width), lambda b, q: (b, q, 0))

    log2e = math.log2(math.e)
    kern = functools.partial(_prep_kernel, mla_scale=(MLA_NOPE + MLA_ROPE) ** -0.5 * log2e,
                             diff_scale=DIFF_QK ** -0.5 * log2e)
    vx = MLA_HEADS * 2 * LANES
    widths = (MLA_HEADS * MLA_HEAD_PAD, MLA_HEADS * MLA_HEAD_PAD, vx, 512, 512, vx)
    return pl.pallas_call(
        kern,
        grid=(bsz, nt),
        in_specs=[seg(MLA_KV_RANK, SEG_CKV), seg(LANES, SEG_KPE), seg(MLA_Q_RANK, SEG_CQ), seg(512, SEG_DQ),
                  seg(512, SEG_DK), seg(512, SEG_DV),
                  pl.BlockSpec((TILE_M, LANES), lambda b, q: (q, 0)), pl.BlockSpec((TILE_M, LANES), lambda b, q: (q, 0)),
                  _const_spec(g_kv.shape), _const_spec(g_q.shape), _const_spec(w_ukv.shape), _const_spec(w_uq.shape)],
        out_specs=[out(w) for w in widths],
        out_shape=[jax.ShapeDtypeStruct((bsz, l_all, w), BF16) for w in widths],
        compiler_params=_params("parallel", "parallel"),
        name="attn_prep",
    )(p, p, p, p, p, p, cos_t, sin_t, g_kv, g_q, w_ukv, w_uq)


KEY_CHUNK = 2048


def _key_chunks(n_ctx, n_all):
    step = math.gcd(n_all - n_ctx, KEY_CHUNK)
    return [(0, n_ctx)] + [(lo, step) for lo in range(n_ctx, n_all, step)]


def _softmax_pv(q, k_ref, v_ref, chunks):
    def scores(chunk):
        lo, n = chunk
        return lax.dot_general(q, k_ref[0, lo:lo + n, :], (((1,), (1,)), ((), ())), preferred_element_type=F32)

    m = acc = None
    s_next = scores(chunks[0])
    for c, (lo, n) in enumerate(chunks):
        s = s_next
        if c + 1 < len(chunks):
            s_next = scores(chunks[c + 1])
        m_new = jnp.max(s, axis=-1, keepdims=True)
        if m is not None:
            m_new = jnp.maximum(m, m_new)
        pv = jnp.dot(jnp.exp2(s - m_new).astype(BF16), v_ref[0, lo:lo + n, :], preferred_element_type=F32)
        acc = pv if acc is None else acc * jnp.exp2(m - m_new) + pv
        m = m_new
    return acc


def _normalised(acc, width):
    return acc[:, :width] / acc[:, width:width + 1]


def _mla_kernel(q_ref, k_ref, v_ref, o_ref, *, n_ctx):
    q = q_ref[0]
    chunks = _key_chunks(n_ctx, k_ref.shape[1])

    def run(chunks):
        o_ref[0] = _normalised(_softmax_pv(q, k_ref, v_ref, chunks), MLA_V).astype(o_ref.dtype)

    is_ctx = pl.program_id(2) < n_ctx // TILE_M
    pl.when(is_ctx)(lambda: run(chunks[:1]))
    pl.when(jnp.logical_not(is_ctx))(lambda: run(chunks))


def _mla_attention(mq, mk, mv, n_ctx):
    bsz, l_all, _ = mq.shape
    nt = l_all // TILE_M
    return pl.pallas_call(
        functools.partial(_mla_kernel, n_ctx=n_ctx),
        grid=(bsz, MLA_HEADS, nt),
        in_specs=[pl.BlockSpec((1, TILE_M, MLA_HEAD_PAD), lambda b, h, q: (b, q, h)),
                  pl.BlockSpec((1, l_all, MLA_HEAD_PAD), lambda b, h, q: (b, 0, h)),
                  pl.BlockSpec((1, l_all, 2 * LANES), lambda b, h, q: (b, 0, h))],
        out_specs=pl.BlockSpec((1, TILE_M, MLA_V), lambda b, h, q: (b, q, h)),
        out_shape=jax.ShapeDtypeStruct((bsz, l_all, MLA_HEADS * MLA_V), BF16),
        compiler_params=_params("parallel", "parallel", "parallel"),
        name="mla_attn",
    )(mq, mk, mv)


def _diff_kernel(q_ref, k_ref, v_ref, lam_ref, g_ref, o_ref, *, n_ctx, lam_init):
    lv = lam_ref[...]
    lam = (jnp.exp(jnp.sum(lv[0:1] * lv[1:2], axis=-1, keepdims=True))
           - jnp.exp(jnp.sum(lv[2:3] * lv[3:4], axis=-1, keepdims=True)) + lam_init)
    q = q_ref[0]
    lane = lax.broadcasted_iota(I32, q.shape, 1)
    q12 = jnp.concatenate([jnp.where(lane < DIFF_QK, q, jnp.zeros_like(q)),
                           jnp.where(lane >= DIFF_QK, q, jnp.zeros_like(q))], axis=0)
    chunks = _key_chunks(n_ctx, k_ref.shape[1])
    tq = q.shape[0]

    def run(chunks):
        acc = _softmax_pv(q12, k_ref, v_ref, chunks)
        o = _normalised(acc[:tq], DIFF_V) - lam * _normalised(acc[tq:], DIFF_V)
        o_ref[0] = (_rmsnorm(o, g_ref[...]) * (1.0 - lam_init)).astype(o_ref.dtype)

    is_ctx = pl.program_id(2) < n_ctx // TILE_M
    pl.when(is_ctx)(lambda: run(chunks[:1]))
    pl.when(jnp.logical_not(is_ctx))(lambda: run(chunks))


def _diff_attention(dqr, dkr, dvx, lam_vec, g_sub, n_ctx, lam_init):
    bsz, l_all, _ = dqr.shape
    nt = l_all // TILE_M
    w = 2 * DIFF_QK
    return pl.pallas_call(
        functools.partial(_diff_kernel, n_ctx=n_ctx, lam_init=lam_init),
        grid=(bsz, DIFF_HEADS, nt),
        in_specs=[pl.BlockSpec((1, TILE_M, w), lambda b, h, q: (b, q, h)),
                  pl.BlockSpec((1, l_all, w), lambda b, h, q: (b, 0, h)),
                  pl.BlockSpec((1, l_all, 2 * LANES), lambda b, h, q: (b, 0, h)),
                  _const_spec(lam_vec.shape), _const_spec(g_sub.shape)],
        out_specs=pl.BlockSpec((1, TILE_M, DIFF_V), lambda b, h, q: (b, q, h)),
        out_shape=jax.ShapeDtypeStruct((bsz, l_all, DIFF_HEADS * DIFF_V), BF16),
        compiler_params=_params("parallel", "parallel", "parallel"),
        name="diff_attn",
    )(dqr, dkr, dvx, lam_vec, g_sub)


def _gelu_tanh(x):
    return 0.5 * x * (1.0 + jnp.tanh(math.sqrt(2.0 / math.pi) * (x + 0.044715 * (x * x * x))))


def _gmlp_kernel(z_ref, g_ref, b_ref, ws_ref, bst_ref, o_ref):
    z = _gelu_tanh(z_ref[0].astype(F32))
    u = z[:, :GMLP_W]
    v = _layernorm(z[:, GMLP_W:], g_ref[...], b_ref[...]).astype(BF16)
    gw = GMLP_W // GMLP_GROUPS
    for c in range(z.shape[0] // CHUNK):
        rows = slice(c * CHUNK, (c + 1) * CHUNK)
        for g in range(GMLP_GROUPS):
            cols = slice(g * gw, (g + 1) * gw)
            s = jnp.dot(ws_ref[g], v[rows, cols], preferred_element_type=F32) + bst_ref[:, g:g + 1]
            o_ref[0, rows, cols] = (u[rows, cols] * s).astype(o_ref.dtype)


def _gmlp(p, ln_g, ln_b, ws, bs_t):
    bsz, l_all, _ = p.shape
    nt = l_all // TILE_M
    return pl.pallas_call(
        _gmlp_kernel,
        grid=(bsz, nt),
        in_specs=[pl.BlockSpec((1, TILE_M, 2 * GMLP_W), lambda b, q: (b, q, SEG_GM // (2 * GMLP_W))),
                  _const_spec(ln_g.shape), _const_spec(ln_b.shape), _const_spec(ws.shape), _const_spec(bs_t.shape)],
        out_specs=pl.BlockSpec((1, TILE_M, GMLP_W), lambda b, q: (b, q, 0)),
        out_shape=jax.ShapeDtypeStruct((bsz, l_all, GMLP_W), BF16),
        compiler_params=_params("parallel", "parallel"),
        name="gmlp",
    )(p, ln_g, ln_b, ws, bs_t)


def _pool_kernel(p_ref, w_ref, sc_ref, o_ref, pad_ref, *, n_ctx):
    n_all = p_ref.shape[1]
    gw = LANES
    t = lax.broadcasted_iota(I32, (n_all, gw), 0)
    seq_lo = jnp.where(t < n_ctx, 0, n_ctx)
    seq_hi = jnp.where(t < n_ctx, n_ctx, n_all)
    pad_ref[0:POOL_PAD, :] = jnp.zeros((POOL_PAD, gw), F32)
    pad_ref[POOL_PAD + n_all:, :] = jnp.zeros((POOL_PAD, gw), F32)
    for i, win in enumerate(POOL_WINDOWS):
        cols = slice(i * gw, (i + 1) * gw)
        x = p_ref[0, :, cols].astype(F32)
        pad_ref[POOL_PAD:POOL_PAD + n_all, :] = x
        acc = jnp.zeros((n_all, gw), F32)
        cnt = jnp.zeros((n_all, gw), F32)
        for d in range(-(win // 2), win - win // 2):
            valid = jnp.logical_and(t + d >= seq_lo, t + d < seq_hi)
            acc = acc + jnp.where(valid, pad_ref[POOL_PAD + d:POOL_PAD + d + n_all, :], 0.0)
            cnt = cnt + valid.astype(F32)
        resid = (acc / cnt - x).astype(BF16)
        o = jnp.dot(resid, w_ref[i], preferred_element_type=F32) * sc_ref[:, cols]
        o_ref[0, :, cols] = o.astype(o_ref.dtype)


def _pool(p, pool_w, pool_scale, n_ctx):
    bsz, l_all, _ = p.shape
    width = len(POOL_WINDOWS) * LANES
    return pl.pallas_call(
        functools.partial(_pool_kernel, n_ctx=n_ctx),
        grid=(bsz,),
        in_specs=[pl.BlockSpec((1, l_all, width), lambda b: (b, 0, SEG_PL // width)),
                  _const_spec(pool_w.shape), _const_spec(pool_scale.shape)],
        out_specs=pl.BlockSpec((1, l_all, width), lambda b: (b, 0, 0)),
        out_shape=jax.ShapeDtypeStruct((bsz, l_all, width), BF16),
        scratch_shapes=[pltpu.VMEM((l_all + 2 * POOL_PAD, LANES), F32)],
        compiler_params=_params("parallel"),
        name="pool",
    )(p, pool_w, pool_scale)


def _merge_kernel(b0_ref, b1_ref, b2_ref, b3_ref, gl_ref, x_ref, g1_ref, wb_ref, wo_ref, lg_ref, lb_ref, o_ref, *,
                  alpha):
    d = x_ref.shape[2]
    merged = None
    for n, br in enumerate((b0_ref, b1_ref, b2_ref, b3_ref)):
        t = jnp.dot(br[0], wb_ref[n], preferred_element_type=F32)
        term = _sigmoid_t(gl_ref[0, :, n * d:(n + 1) * d].astype(F32)) * t
        merged = term if merged is None else merged + term
    y = jnp.dot(merged.astype(BF16), wo_ref[...], preferred_element_type=F32)
    o_ref[0] = _layernorm(alpha * x_ref[0] + g1_ref[0] * y, lg_ref[...], lb_ref[...])


def _merge(branches, gl, x, g1, w_branch, w_out, ln_g, ln_b, n_ctx_tiles, alpha):
    bsz, l_all, d = x.shape
    nt = l_all // TILE_M
    row = lambda width: pl.BlockSpec((1, TILE_M, width), lambda b, q: (b, q, 0))
    return pl.pallas_call(
        functools.partial(_merge_kernel, alpha=alpha),
        grid=(bsz, nt),
        in_specs=[row(BRANCH_W)] * N_BRANCH + [row(N_BRANCH * d), row(d), _mod_spec(n_ctx_tiles, d, "bq"),
                                               _const_spec(w_branch.shape), _const_spec(w_out.shape),
                                               _const_spec(ln_g.shape), _const_spec(ln_b.shape)],
        out_specs=row(d),
        out_shape=jax.ShapeDtypeStruct((bsz, l_all, d), F32),
        compiler_params=_params("parallel", "parallel"),
        name="merge",
    )(*branches, gl, x, g1, w_branch, w_out, ln_g, ln_b)


def _pack_words(y):
    d = y.shape[1]
    bits = pltpu.bitcast(y.astype(BF16).astype(F32), U32)
    return (bits[:, :d // 2] >> 16) | bits[:, d // 2:]


def _unpack_words(w):
    return pltpu.bitcast(w << 16, F32), pltpu.bitcast(w & jnp.uint32(0xFFFF0000), F32)


def _route_kernel(x_ref, sc_ref, sh_ref, rwh_ref, rwl_ref, rb_ref, hp_ref, idx_ref, wt_ref):
    h = x_ref[0] * (1.0 + sc_ref[0]) + sh_ref[0]
    h_hi = h.astype(BF16)
    h_lo = (h - h_hi.astype(F32)).astype(BF16)
    logits = (jnp.dot(h_hi, rwh_ref[...], preferred_element_type=F32)
              + jnp.dot(h_lo, rwh_ref[...], preferred_element_type=F32)
              + jnp.dot(h_hi, rwl_ref[...], preferred_element_type=F32))
    scores = _sigmoid(logits)
    biased = scores + rb_ref[...]
    lane = lax.broadcasted_iota(I32, biased.shape, 1)
    neg = jnp.float32(-jnp.inf)
    best = None
    for g in range(N_GROUPS):
        in_g = jnp.logical_and(lane >= g * EXPERTS_PER_GROUP, lane < (g + 1) * EXPERTS_PER_GROUP)
        a = jnp.where(in_g, biased, neg)
        m1 = jnp.max(a, axis=-1, keepdims=True)
        i1 = jnp.min(jnp.where(a == m1, lane, LANES), axis=-1, keepdims=True)
        a2 = jnp.where(lane == i1, neg, a)
        m2 = jnp.max(a2, axis=-1, keepdims=True)
        i2 = jnp.min(jnp.where(a2 == m2, lane, LANES), axis=-1, keepdims=True)
        gs = m1 + m2
        if best is None:
            best = (gs, i1, i2)
        else:
            take = gs > best[0]
            best = (jnp.where(take, gs, best[0]), jnp.where(take, i1, best[1]), jnp.where(take, i2, best[2]))
    _, e0, e1 = best
    w0 = jnp.sum(jnp.where(lane == e0, scores, 0.0), axis=-1, keepdims=True)
    w1 = jnp.sum(jnp.where(lane == e1, scores, 0.0), axis=-1, keepdims=True)
    tot = w0 + w1
    idx_ref[...] = jnp.where(lane == 0, e0, jnp.where(lane == 1, e1, 0))
    wt_ref[...] = jnp.where(lane == 0, w0 / tot, jnp.where(lane == 1, w1 / tot, 0.0))
    hp_ref[...] = _pack_words(h)


def _route(x, sc, sh, rw_hi, rw_lo, rb, n_ctx_tiles):
    bsz, l_all, d = x.shape
    nt = l_all // TILE_M
    t_all = bsz * l_all
    flat = lambda b, q: (b * nt + q, 0)
    return pl.pallas_call(
        _route_kernel,
        grid=(bsz, nt),
        in_specs=[pl.BlockSpec((1, TILE_M, d), lambda b, q: (b, q, 0)),
                  _mod_spec(n_ctx_tiles, d, "bq"), _mod_spec(n_ctx_tiles, d, "bq"),
                  _const_spec(rw_hi.shape), _const_spec(rw_lo.shape), _const_spec(rb.shape)],
        out_specs=[pl.BlockSpec((TILE_M, d // 2), flat), pl.BlockSpec((TILE_M, LANES), flat),
                   pl.BlockSpec((TILE_M, LANES), flat)],
        out_shape=[jax.ShapeDtypeStruct((t_all, d // 2), U32), jax.ShapeDtypeStruct((t_all, LANES), I32),
                   jax.ShapeDtypeStruct((t_all, LANES), F32)],
        compiler_params=_params("parallel", "parallel"),
        name="moe_route",
    )(x, sc, sh, rw_hi, rw_lo, rb)


def _route_tables(e_pairs, t_all, n_tiles):
    onehot = (e_pairs[:, None] == jnp.arange(N_EXPERTS, dtype=I32)[None, :]).astype(I32)
    csum = jnp.cumsum(onehot, axis=0)
    rank = jnp.sum((csum - onehot) * onehot, axis=1)
    counts = csum[-1]
    padded = ((counts + TILE_M - 1) // TILE_M) * TILE_M
    ends = jnp.cumsum(padded)
    pos = (ends - padded)[e_pairs] + rank
    tok = jnp.tile(jnp.arange(t_all, dtype=I32), 2)
    src_tok = jnp.zeros(((n_tiles + 1) * TILE_M,), I32).at[pos].set(tok)
    tile_start = jnp.arange(n_tiles, dtype=I32) * TILE_M
    tile_expert = jnp.minimum(jnp.sum((tile_start[:, None] >= ends[None, :]).astype(I32), axis=1), N_EXPERTS - 1)
    pos = jnp.concatenate([pos.astype(I32), jnp.zeros((TILE_M,), I32)])
    return src_tok, pos, tile_expert


def _start_row_gather(idx_ref, base, src_ref, dst_ref, sem, n_rows, straight_line=False):
    def start(r):
        pltpu.make_async_copy(src_ref.at[pl.ds(idx_ref[base + r], 1)], dst_ref.at[pl.ds(r, 1)], sem).start()

    if straight_line:
        for r in range(n_rows):
            start(r)
    else:
        lax.fori_loop(0, n_rows, lambda r, c: (start(r), c)[1], 0, unroll=8)


def _wait_row_gather(src_ref, dst_ref, sem):
    pltpu.make_async_copy(src_ref.at[pl.ds(0, dst_ref.shape[0])], dst_ref, sem).wait()


def _double_buffered_step(i, n_steps, bufs, sem, src_ref, fetch, compute):
    pl.when(i == 0)(lambda: fetch(0, bufs[0], sem.at[0], False))
    for parity in range(2):
        cur, nxt = bufs[parity], bufs[1 - parity]

        @pl.when(i % 2 == parity)
        def _(cur=cur, nxt=nxt, parity=parity):
            _wait_row_gather(src_ref, cur, sem.at[parity])
            fetch(i + 1, nxt, sem.at[1 - parity], True)
            compute(cur)

    last = n_steps % 2
    pl.when(i == n_steps - 1)(lambda: _wait_row_gather(src_ref, bufs[last], sem.at[last]))


def _expert_kernel(te_ref, tok_ref, hp_ref, w1_ref, w3_ref, w2_ref, o_ref, rows0_ref, rows1_ref, w1b_ref, w3b_ref,
                   w2b_ref, sem, *, n_tiles):
    i = pl.program_id(0)

    @pl.when(jnp.logical_or(i == 0, te_ref[i] != te_ref[jnp.maximum(i - 1, 0)]))
    def _():
        w1b_ref[...] = w1_ref[0, 0].astype(BF16)
        w3b_ref[...] = w3_ref[0, 0].astype(BF16)
        w2b_ref[...] = w2_ref[0, 0].astype(BF16)

    def fetch(tile, dst_ref, dst_sem, straight_line):
        _start_row_gather(tok_ref, tile * TILE_M, hp_ref, dst_ref, dst_sem, TILE_M, straight_line)

    def compute(rows_ref):
        half = w1b_ref.shape[0] // 2
        lo, hi = _unpack_words(rows_ref[...])
        lo, hi = lo.astype(BF16), hi.astype(BF16)
        a = (jnp.dot(lo, w1b_ref[:half, :], preferred_element_type=F32)
             + jnp.dot(hi, w1b_ref[half:, :], preferred_element_type=F32))
        b = (jnp.dot(lo, w3b_ref[:half, :], preferred_element_type=F32)
             + jnp.dot(hi, w3b_ref[half:, :], preferred_element_type=F32))
        hid = (a * _sigmoid_t(a) * b).astype(BF16)
        o_ref[...] = _pack_words(jnp.dot(hid, w2b_ref[...], preferred_element_type=F32))

    _double_buffered_step(i, n_tiles, (rows0_ref, rows1_ref), sem, hp_ref, fetch, compute)


def _experts(hp, src_tok, tile_expert, w1, w3, w2, layer):
    n_tiles = tile_expert.shape[0]
    _, _, d, de = w1.shape
    rows = pltpu.VMEM((TILE_M, d // 2), U32)
    return pl.pallas_call(
        functools.partial(_expert_kernel, n_tiles=n_tiles),
        grid_spec=pltpu.PrefetchScalarGridSpec(
            num_scalar_prefetch=2,
            grid=(n_tiles,),
            in_specs=[pl.BlockSpec(memory_space=pl.ANY),
                      pl.BlockSpec((1, 1, d, de), lambda i, te, tok: (layer, te[i], 0, 0)),
                      pl.BlockSpec((1, 1, d, de), lambda i, te, tok: (layer, te[i], 0, 0)),
                      pl.BlockSpec((1, 1, de, d), lambda i, te, tok: (layer, te[i], 0, 0))],
            out_specs=pl.BlockSpec((TILE_M, d // 2), lambda i, te, tok: (i, 0)),
            scratch_shapes=[rows, rows, pltpu.VMEM((d, de), BF16), pltpu.VMEM((d, de), BF16),
                            pltpu.VMEM((de, d), BF16), pltpu.SemaphoreType.DMA((2,))]),
        out_shape=jax.ShapeDtypeStruct((n_tiles * TILE_M, d // 2), U32),
        compiler_params=_params("arbitrary"),
        name="moe_experts",
    )(tile_expert, src_tok, hp, w1, w3, w2)


def _combine_kernel(pos_ref, ys_ref, wt_ref, x_ref, g2_ref, lg_ref, lb_ref, o_ref, rows0_ref, rows1_ref, sem, *,
                    alpha, t_all, n_tiles):
    def fetch(tile, dst_ref, dst_sem, straight_line):
        for k in range(2):
            _start_row_gather(pos_ref, k * t_all + tile * TILE_M, ys_ref, dst_ref.at[pl.ds(k * TILE_M, TILE_M)],
                              dst_sem, TILE_M, straight_line)

    def compute(rows_ref):
        w = wt_ref[...]
        lo0, hi0 = _unpack_words(rows_ref[:TILE_M, :])
        lo1, hi1 = _unpack_words(rows_ref[TILE_M:, :])
        w0, w1 = w[:, 0:1], w[:, 1:2]
        y = jnp.concatenate([w0 * lo0 + w1 * lo1, w0 * hi0 + w1 * hi1], axis=1)
        o_ref[0] = _layernorm(alpha * x_ref[0] + g2_ref[0] * y, lg_ref[...], lb_ref[...])

    _double_buffered_step(pl.program_id(0), n_tiles, (rows0_ref, rows1_ref), sem, ys_ref, fetch, compute)


def _combine(ys, pos, wt, x, g2, ln_g, ln_b, n_ctx_tiles, alpha):
    bsz, l_all, d = x.shape
    nt = l_all // TILE_M
    rows = pltpu.VMEM((2 * TILE_M, d // 2), U32)
    return pl.pallas_call(
        functools.partial(_combine_kernel, alpha=alpha, t_all=bsz * l_all, n_tiles=bsz * nt),
        grid_spec=pltpu.PrefetchScalarGridSpec(
            num_scalar_prefetch=1,
            grid=(bsz * nt,),
            in_specs=[pl.BlockSpec(memory_space=pl.ANY),
                      pl.BlockSpec((TILE_M, LANES), lambda i, pos: (i, 0)),
                      pl.BlockSpec((1, TILE_M, d), lambda i, pos: (i // nt, i % nt, 0)),
                      pl.BlockSpec((1, 1, d), lambda i, pos: (2 * (i // nt) + (i % nt >= n_ctx_tiles).astype(I32), 0, 0)),
                      pl.BlockSpec(ln_g.shape, lambda i, pos: (0, 0), pipeline_mode=pl.Buffered(1)),
                      pl.BlockSpec(ln_b.shape, lambda i, pos: (0, 0), pipeline_mode=pl.Buffered(1))],
            out_specs=pl.BlockSpec((1, TILE_M, d), lambda i, pos: (i // nt, i % nt, 0)),
            scratch_shapes=[rows, rows, pltpu.SemaphoreType.DMA((2,))]),
        out_shape=jax.ShapeDtypeStruct((bsz, l_all, d), F32),
        compiler_params=_params("arbitrary"),
        name="moe_combine",
    )(pos, ys, wt, x, g2, ln_g, ln_b)


def _rope_tables(n_ctx, n_lat):
    rows = n_lat // GRID_W
    row = jnp.repeat(jnp.arange(rows, dtype=F32), GRID_W)
    col = jnp.tile(jnp.arange(GRID_W, dtype=F32), rows)
    n_freq = ROPE_DIM // 4
    inv = ROPE_BASE ** (-jnp.arange(n_freq, dtype=F32) / n_freq)
    ang = jnp.concatenate([row[:, None] * inv, col[:, None] * inv], axis=-1)
    cos = jnp.concatenate([jnp.ones((n_ctx, ROPE_DIM // 2), F32), jnp.cos(ang)], axis=0)
    sin = jnp.concatenate([jnp.zeros((n_ctx, ROPE_DIM // 2), F32), jnp.sin(ang)], axis=0)
    reps = LANES // ROPE_DIM
    return jnp.tile(jnp.concatenate([cos, cos], axis=1), (1, reps)), jnp.tile(jnp.concatenate([-sin, sin], axis=1), (1, reps))


def _split_w_in(w):
    d = w.shape[0]
    o = 0
    segs = {}
    for name, width in (("ckv", MLA_KV_RANK), ("kpe", MLA_ROPE), ("dk", 512), ("dv", 512), ("cq", MLA_Q_RANK),
                        ("dq", 512), ("gm", 2 * GMLP_W), ("pl", 512), ("gate", N_BRANCH * d)):
        segs[name] = w[:, o:o + width]
        o += width
    small = jnp.concatenate([segs["dk"], segs["dv"], segs["dq"], segs["pl"], segs["gm"], segs["ckv"], segs["kpe"],
                             jnp.zeros((d, LANES - MLA_ROPE), w.dtype), segs["cq"]], axis=1)
    return small.astype(BF16), segs["gate"].astype(BF16)


def _forward(x, c, ctx, c_ctx, ada_w, ada_b, w_in, mla_q_norm, mla_kv_norm, mla_w_uq, mla_w_ukv, diff_lambda,
             diff_subln, gmlp_ln_g, gmlp_ln_b, gmlp_ws, gmlp_bs, pool_w, pool_scale, w_branch, w_out, ln1_g, ln1_b,
             ln2_g, ln2_b, router_w, router_bias, moe_w1, moe_w3, moe_w2):
    bsz, n_lat, d = x.shape
    n_ctx = ctx.shape[1]
    depth = w_in.shape[0]
    l_all = n_ctx + n_lat
    t_all = bsz * l_all
    n_ctx_tiles = n_ctx // TILE_M
    alpha = (2 * depth) ** 0.25
    n_row_tiles = (2 * t_all) // TILE_M + N_EXPERTS

    xs = jnp.concatenate([ctx, x], axis=1)
    cos_t, sin_t = _rope_tables(n_ctx, n_lat)

    c_rows = ((bsz + 1 + SUBLANES - 1) // SUBLANES) * SUBLANES
    c_all = jnp.zeros((c_rows, d), F32).at[:bsz].set(c).at[bsz].set(c_ctx)
    mod = _ada(c_all, ada_w, ada_b)
    pick = jnp.stack([jnp.full((bsz,), bsz, I32), jnp.arange(bsz, dtype=I32)], axis=1).reshape(-1)
    mod = mod[:, pick].reshape(depth, 2 * bsz, 1, 6, d)

    rw = jnp.zeros((d, LANES), F32).at[:, :N_EXPERTS].set(router_w)
    rw_hi = rw.astype(BF16)
    rw_lo = (rw - rw_hi.astype(F32)).astype(BF16)
    rb = jnp.zeros((1, LANES), F32).at[0, :N_EXPERTS].set(router_bias)
    w_in16 = w_in.astype(BF16)

    for l in range(depth):
        sh1, sc1, g1, sh2, sc2, g2 = (mod[l, :, :, k] for k in range(6))
        lam_init = 0.8 - 0.6 * math.exp(-0.3 * l)
        w_small, w_gate = _split_w_in(w_in16[l])
        w_ukv = mla_w_ukv[l].reshape(MLA_KV_RANK, MLA_HEADS, 2, MLA_NOPE).transpose(0, 2, 1, 3).reshape(MLA_KV_RANK, -1)
        w_uq = jnp.pad(mla_w_uq[l].reshape(MLA_Q_RANK, MLA_HEADS, MLA_NOPE + MLA_ROPE),
                       ((0, 0), (0, 0), (0, MLA_HEAD_PAD - MLA_NOPE - MLA_ROPE))).reshape(MLA_Q_RANK, -1)

        p = _inproj(xs, sc1, sh1, w_small, 1280, n_ctx_tiles, "inproj_small")
        gl = _inproj(xs, sc1, sh1, w_gate, 2048, n_ctx_tiles, "inproj_gate")
        mq, mk, mv, dqr, dkr, dvx = _prep(p, cos_t, sin_t, mla_kv_norm[l][None], mla_q_norm[l][None],
                                          w_ukv.astype(BF16), w_uq.astype(BF16))
        mla_out = _mla_attention(mq, mk, mv, n_ctx)
        d_out = _diff_attention(dqr, dkr, dvx, diff_lambda[l], diff_subln[l][None], n_ctx, lam_init)
        g_out = _gmlp(p, gmlp_ln_g[l][None], gmlp_ln_b[l][None], gmlp_ws[l].astype(BF16), gmlp_bs[l].T)
        p_out = _pool(p, pool_w[l].astype(BF16), pool_scale[l][None], n_ctx)
        xs = _merge((mla_out, d_out, g_out, p_out), gl, xs, g1, w_branch[l].astype(BF16), w_out[l].astype(BF16),
                    ln1_g[l][None], ln1_b[l][None], n_ctx_tiles, alpha)

        hp, idx, wt = _route(xs, sc2, sh2, rw_hi, rw_lo, rb, n_ctx_tiles)
        e_pairs = idx[:, :2].T.reshape(-1)
        src_tok, pos, tile_expert = _route_tables(e_pairs, t_all, n_row_tiles)
        ys = _experts(hp, src_tok, tile_expert, moe_w1, moe_w3, moe_w2, l)
        xs = _combine(ys, pos, wt, xs, g2, ln2_g[l][None], ln2_b[l][None], n_ctx_tiles, alpha)
    return xs[:, n_ctx:]


_forward_jit = jax.jit(_forward)


def kernel(x, c, ctx, c_ctx, ada_w, ada_b, w_in, mla_q_norm, mla_kv_norm, mla_w_uq, mla_w_ukv, diff_lambda, diff_subln, gmlp_ln_g, gmlp_ln_b, gmlp_ws, gmlp_bs, pool_w, pool_scale, w_branch, w_out, ln1_g, ln1_b, ln2_g, ln2_b, router_w, router_bias, moe_w1, moe_w3, moe_w2):
    return _forward_jit(x, c, ctx, c_ctx, ada_w, ada_b, w_in, mla_q_norm, mla_kv_norm, mla_w_uq, mla_w_ukv,
                        diff_lambda, diff_subln, gmlp_ln_g, gmlp_ln_b, gmlp_ws, gmlp_bs, pool_w, pool_scale,
                        w_branch, w_out, ln1_g, ln1_b, ln2_g, ln2_b, router_w, router_bias, moe_w1, moe_w3, moe_w2)
```

```python
import functools
import math

import jax
import jax.numpy as jnp
from jax import lax
from jax.experimental import pallas as pl
from jax.experimental.pallas import tpu as pltpu

F32 = jnp.float32
BF16 = jnp.bfloat16
U32 = jnp.uint32
I32 = jnp.int32

LANES = 128
SUBLANES = 8
TILE_M = 256
VMEM_LIMIT = 56 << 20

GRID_W = 64
ROPE_DIM = 64
ROPE_BASE = 10000.0
MLA_HEADS = 4
MLA_NOPE = 128
MLA_ROPE = 64
MLA_V = 128
MLA_Q_RANK = 384
MLA_KV_RANK = 256
MLA_HEAD_PAD = 256
DIFF_HEADS = 4
DIFF_QK = 64
DIFF_V = 128
CHUNK = 128
GMLP_GROUPS = 4
GMLP_W = 512
POOL_WINDOWS = (2, 4, 8, 16)
POOL_PAD = 16
N_BRANCH = 4
BRANCH_W = 512
N_EXPERTS = 32
N_GROUPS = 4
EXPERTS_PER_GROUP = 8
D_EXPERT = 512
LN_EPS = 1e-5
RMS_EPS = 1e-6

SEG_DK, SEG_DV, SEG_DQ, SEG_PL, SEG_GM, SEG_CKV, SEG_KPE, SEG_CQ = 0, 512, 1024, 1536, 2048, 3072, 3328, 3456
N_SMALL = 3840


def _params(*sem):
    return pltpu.CompilerParams(dimension_semantics=sem, vmem_limit_bytes=VMEM_LIMIT)


def _const_spec(shape):
    nd = len(shape)
    return pl.BlockSpec(shape, lambda *_: (0,) * nd, pipeline_mode=pl.Buffered(1))


def _sigmoid(x):
    return 1.0 / (1.0 + jnp.exp(-x))


def _sigmoid_t(x):
    return 0.5 * jnp.tanh(0.5 * x) + 0.5


def _layernorm(x, g, b):
    mu = jnp.mean(x, axis=-1, keepdims=True)
    xc = x - mu
    var = jnp.mean(xc * xc, axis=-1, keepdims=True)
    return xc * lax.rsqrt(var + LN_EPS) * g + b


def _rmsnorm(x, g):
    return x * lax.rsqrt(jnp.mean(x * x, axis=-1, keepdims=True) + RMS_EPS) * g


def _rope(x, cos_t, sin_t):
    lane = lax.broadcasted_iota(I32, (x.shape[0], LANES), 1)
    first_half = (lane % ROPE_DIM) < (ROPE_DIM // 2)
    outs = []
    for j in range(x.shape[1] // LANES):
        xc = x[:, j * LANES:(j + 1) * LANES]
        partner = jnp.where(first_half, pltpu.roll(xc, LANES - ROPE_DIM // 2, 1), pltpu.roll(xc, ROPE_DIM // 2, 1))
        outs.append(xc * cos_t + partner * sin_t)
    return outs[0] if len(outs) == 1 else jnp.concatenate(outs, axis=1)


def _ada_kernel(c_ref, w_ref, b_ref, o_ref):
    c = c_ref[...]
    s = (c * _sigmoid(c)).astype(BF16)
    o_ref[0] = jnp.dot(s, w_ref[0].astype(BF16), preferred_element_type=F32) + b_ref[0]


def _ada(c_all, ada_w, ada_b):
    depth, d, n6 = ada_w.shape
    tn = 1024
    rows = c_all.shape[0]
    return pl.pallas_call(
        _ada_kernel,
        grid=(depth, n6 // tn),
        in_specs=[pl.BlockSpec((rows, d), lambda l, j: (0, 0)),
                  pl.BlockSpec((1, d, tn), lambda l, j: (l, 0, j)),
                  pl.BlockSpec((1, 1, tn), lambda l, j: (l, 0, j))],
        out_specs=pl.BlockSpec((1, rows, tn), lambda l, j: (l, 0, j)),
        out_shape=jax.ShapeDtypeStruct((depth, rows, n6), F32),
        compiler_params=_params("parallel", "parallel"),
        name="ada",
    )(c_all, ada_w, ada_b.reshape(depth, 1, n6))


def _inproj_kernel(x_ref, sc_ref, sh_ref, w_ref, o_ref):
    h = (x_ref[0] * (1.0 + sc_ref[0]) + sh_ref[0]).astype(BF16)
    o_ref[0] = jnp.dot(h, w_ref[...], preferred_element_type=F32).astype(o_ref.dtype)


def _mod_spec(n_ctx_tiles, d, order):
    if order == "jbq":
        return pl.BlockSpec((1, 1, d), lambda j, b, q: (2 * b + (q >= n_ctx_tiles).astype(I32), 0, 0))
    return pl.BlockSpec((1, 1, d), lambda b, q: (2 * b + (q >= n_ctx_tiles).astype(I32), 0, 0))


def _inproj(x, sc, sh, w, tn, n_ctx_tiles, name):
    bsz, l_all, d = x.shape
    n = w.shape[1]
    nt = l_all // TILE_M
    return pl.pallas_call(
        _inproj_kernel,
        grid=(n // tn, bsz, nt),
        in_specs=[pl.BlockSpec((1, TILE_M, d), lambda j, b, q: (b, q, 0)),
                  _mod_spec(n_ctx_tiles, d, "jbq"), _mod_spec(n_ctx_tiles, d, "jbq"),
                  pl.BlockSpec((d, tn), lambda j, b, q: (0, j))],
        out_specs=pl.BlockSpec((1, TILE_M, tn), lambda j, b, q: (b, q, j)),
        out_shape=jax.ShapeDtypeStruct((bsz, l_all, n), BF16),
        compiler_params=_params("parallel", "parallel", "parallel"),
        name=name,
    )(x, sc, sh, w)


def _prep_kernel(ckv_ref, kpe_ref, cq_ref, dq_ref, dk_ref, dv_ref, cos_ref, sin_ref, gkv_ref, gq_ref, wukv_ref, wuq_ref,
                 mq_ref, mk_ref, mv_ref, dqr_ref, dkr_ref, dvx_ref, *, mla_scale, diff_scale):
    cos_t = cos_ref[...]
    sin_t = sin_ref[...]
    ckv = _rmsnorm(ckv_ref[0].astype(F32), gkv_ref[...]).astype(BF16)
    kv = jnp.dot(ckv, wukv_ref[...], preferred_element_type=F32)
    kpe = _rope(kpe_ref[0].astype(F32), cos_t, sin_t).astype(BF16)
    lane = lax.broadcasted_iota(I32, (kv.shape[0], LANES), 1)
    ones_col = jnp.where(lane == 0, 1.0, 0.0).astype(BF16)
    for h in range(MLA_HEADS):
        base = h * MLA_HEAD_PAD
        mk_ref[0, :, base:base + MLA_NOPE] = kv[:, h * MLA_NOPE:(h + 1) * MLA_NOPE].astype(BF16)
        mk_ref[0, :, base + MLA_NOPE:base + MLA_HEAD_PAD] = kpe
        vcol = MLA_HEADS * MLA_NOPE + h * MLA_V
        mv_ref[0, :, 2 * h * LANES:(2 * h + 1) * LANES] = kv[:, vcol:vcol + MLA_V].astype(BF16)
        mv_ref[0, :, (2 * h + 1) * LANES:(2 * h + 2) * LANES] = ones_col
        dvx_ref[0, :, 2 * h * LANES:(2 * h + 1) * LANES] = dv_ref[0, :, h * DIFF_V:(h + 1) * DIFF_V]
        dvx_ref[0, :, (2 * h + 1) * LANES:(2 * h + 2) * LANES] = ones_col
    cq = _rmsnorm(cq_ref[0].astype(F32), gq_ref[...]).astype(BF16)
    q = jnp.dot(cq, wuq_ref[...], preferred_element_type=F32)
    for h in range(MLA_HEADS):
        base = h * MLA_HEAD_PAD
        mq_ref[0, :, base:base + MLA_NOPE] = (q[:, base:base + MLA_NOPE] * mla_scale).astype(BF16)
        qpe = _rope(q[:, base + MLA_NOPE:base + MLA_HEAD_PAD], cos_t, sin_t)
        mq_ref[0, :, base + MLA_NOPE:base + MLA_HEAD_PAD] = (qpe * mla_scale).astype(BF16)
    dqr_ref[0] = (_rope(dq_ref[0].astype(F32), cos_t, sin_t) * diff_scale).astype(BF16)
    dkr_ref[0] = _rope(dk_ref[0].astype(F32), cos_t, sin_t).astype(BF16)


def _prep(p, cos_t, sin_t, g_kv, g_q, w_ukv, w_uq):
    bsz, l_all, _ = p.shape
    nt = l_all // TILE_M

    def seg(width, offset):
        return pl.BlockSpec((1, TILE_M, width), lambda b, q: (b, q, offset // width))

    def out(width):
        return pl.BlockSpec((1, TILE_M, width), lambda b, q: (b, q, 0))

    log2e = math.log2(math.e)
    kern = functools.partial(_prep_kernel, mla_scale=(MLA_NOPE + MLA_ROPE) ** -0.5 * log2e,
                             diff_scale=DIFF_QK ** -0.5 * log2e)
    vx = MLA_HEADS * 2 * LANES
    widths = (MLA_HEADS * MLA_HEAD_PAD, MLA_HEADS * MLA_HEAD_PAD, vx, 512, 512, vx)
    return pl.pallas_call(
        kern,
        grid=(bsz, nt),
        in_specs=[seg(MLA_KV_RANK, SEG_CKV), seg(LANES, SEG_KPE), seg(MLA_Q_RANK, SEG_CQ), seg(512, SEG_DQ),
                  seg(512, SEG_DK), seg(512, SEG_DV),
                  pl.BlockSpec((TILE_M, LANES), lambda b, q: (q, 0)), pl.BlockSpec((TILE_M, LANES), lambda b, q: (q, 0)),
                  _const_spec(g_kv.shape), _const_spec(g_q.shape), _const_spec(w_ukv.shape), _const_spec(w_uq.shape)],
        out_specs=[out(w) for w in widths],
        out_shape=[jax.ShapeDtypeStruct((bsz, l_all, w), BF16) for w in widths],
        compiler_params=_params("parallel", "parallel"),
        name="attn_prep",
    )(p, p, p, p, p, p, cos_t, sin_t, g_kv, g_q, w_ukv, w_uq)


KEY_CHUNK = 2048
MLA_HEADS_PER_STEP = 2


def _key_chunks(n_ctx, n_all):
    step = math.gcd(n_all - n_ctx, KEY_CHUNK)
    return [(0, n_ctx)] + [(lo, step) for lo in range(n_ctx, n_all, step)]


def _softmax_pv(streams, k_ref, v_ref, chunks):
    def scores(stream, chunk):
        q, kc, _ = stream
        lo, n = chunk
        return lax.dot_general(q, k_ref[0, lo:lo + n, kc], (((1,), (1,)), ((), ())), preferred_element_type=F32)

    m = [None] * len(streams)
    acc = [None] * len(streams)
    s_next = [scores(st, chunks[0]) for st in streams]
    for c, (lo, n) in enumerate(chunks):
        for i, st in enumerate(streams):
            s = s_next[i]
            if c + 1 < len(chunks):
                s_next[i] = scores(st, chunks[c + 1])
            m_new = jnp.max(s, axis=-1, keepdims=True)
            if m[i] is not None:
                m_new = jnp.maximum(m[i], m_new)
            pv = jnp.dot(jnp.exp2(s - m_new).astype(BF16), v_ref[0, lo:lo + n, st[2]], preferred_element_type=F32)
            acc[i] = pv if acc[i] is None else acc[i] * jnp.exp2(m[i] - m_new) + pv
            m[i] = m_new
    return acc


def _normalised(acc, width):
    return acc[:, :width] / acc[:, width:width + 1]


def _mla_kernel(q_ref, k_ref, v_ref, o_ref, *, n_ctx):
    heads = q_ref.shape[2] // MLA_HEAD_PAD
    cols = [slice(h * MLA_HEAD_PAD, (h + 1) * MLA_HEAD_PAD) for h in range(heads)]
    streams = [(q_ref[0, :, c], c, c) for c in cols]
    chunks = _key_chunks(n_ctx, k_ref.shape[1])

    def run(chunks):
        for h, acc in enumerate(_softmax_pv(streams, k_ref, v_ref, chunks)):
            o_ref[0, :, h * MLA_V:(h + 1) * MLA_V] = _normalised(acc, MLA_V).astype(o_ref.dtype)

    is_ctx = pl.program_id(2) < n_ctx // TILE_M
    pl.when(is_ctx)(lambda: run([(0, n_ctx)]))
    pl.when(jnp.logical_not(is_ctx))(lambda: run(chunks))


def _mla_attention(mq, mk, mv, n_ctx):
    bsz, l_all, _ = mq.shape
    nt = l_all // TILE_M
    hw = MLA_HEADS_PER_STEP * MLA_HEAD_PAD
    return pl.pallas_call(
        functools.partial(_mla_kernel, n_ctx=n_ctx),
        grid=(bsz, MLA_HEADS // MLA_HEADS_PER_STEP, nt),
        in_specs=[pl.BlockSpec((1, TILE_M, hw), lambda b, h, q: (b, q, h)),
                  pl.BlockSpec((1, l_all, hw), lambda b, h, q: (b, 0, h)),
                  pl.BlockSpec((1, l_all, hw), lambda b, h, q: (b, 0, h))],
        out_specs=pl.BlockSpec((1, TILE_M, MLA_HEADS_PER_STEP * MLA_V), lambda b, h, q: (b, q, h)),
        out_shape=jax.ShapeDtypeStruct((bsz, l_all, MLA_HEADS * MLA_V), BF16),
        compiler_params=_params("parallel", "parallel", "parallel"),
        name="mla_attn",
    )(mq, mk, mv)


def _diff_kernel(q_ref, k_ref, v_ref, lam_ref, g_ref, o_ref, *, n_ctx, lam_init):
    lv = lam_ref[...]
    lam = (jnp.exp(jnp.sum(lv[0:1] * lv[1:2], axis=-1, keepdims=True))
           - jnp.exp(jnp.sum(lv[2:3] * lv[3:4], axis=-1, keepdims=True)) + lam_init)
    q = q_ref[0]
    lane = lax.broadcasted_iota(I32, q.shape, 1)
    q12 = jnp.concatenate([jnp.where(lane < DIFF_QK, q, jnp.zeros_like(q)),
                           jnp.where(lane >= DIFF_QK, q, jnp.zeros_like(q))], axis=0)
    chunks = _key_chunks(n_ctx, k_ref.shape[1])
    tq = q.shape[0]

    def run(chunks):
        acc, = _softmax_pv([(q12, slice(None), slice(None))], k_ref, v_ref, chunks)
        o = _normalised(acc[:tq], DIFF_V) - lam * _normalised(acc[tq:], DIFF_V)
        o_ref[0] = (_rmsnorm(o, g_ref[...]) * (1.0 - lam_init)).astype(o_ref.dtype)

    is_ctx = pl.program_id(2) < n_ctx // TILE_M
    pl.when(is_ctx)(lambda: run([(0, n_ctx)]))
    pl.when(jnp.logical_not(is_ctx))(lambda: run(chunks))


def _diff_attention(dqr, dkr, dvx, lam_vec, g_sub, n_ctx, lam_init):
    bsz, l_all, _ = dqr.shape
    nt = l_all // TILE_M
    w = 2 * DIFF_QK
    return pl.pallas_call(
        functools.partial(_diff_kernel, n_ctx=n_ctx, lam_init=lam_init),
        grid=(bsz, DIFF_HEADS, nt),
        in_specs=[pl.BlockSpec((1, TILE_M, w), lambda b, h, q: (b, q, h)),
                  pl.BlockSpec((1, l_all, w), lambda b, h, q: (b, 0, h)),
                  pl.BlockSpec((1, l_all, 2 * LANES), lambda b, h, q: (b, 0, h)),
                  _const_spec(lam_vec.shape), _const_spec(g_sub.shape)],
        out_specs=pl.BlockSpec((1, TILE_M, DIFF_V), lambda b, h, q: (b, q, h)),
        out_shape=jax.ShapeDtypeStruct((bsz, l_all, DIFF_HEADS * DIFF_V), BF16),
        compiler_params=_params("parallel", "parallel", "parallel"),
        name="diff_attn",
    )(dqr, dkr, dvx, lam_vec, g_sub)


def _gelu_tanh(x):
    return 0.5 * x * (1.0 + jnp.tanh(math.sqrt(2.0 / math.pi) * (x + 0.044715 * (x * x * x))))


def _gmlp_kernel(z_ref, g_ref, b_ref, ws_ref, bst_ref, o_ref):
    z = _gelu_tanh(z_ref[0].astype(F32))
    u = z[:, :GMLP_W]
    v = _layernorm(z[:, GMLP_W:], g_ref[...], b_ref[...]).astype(BF16)
    gw = GMLP_W // GMLP_GROUPS
    for c in range(z.shape[0] // CHUNK):
        rows = slice(c * CHUNK, (c + 1) * CHUNK)
        for g in range(GMLP_GROUPS):
            cols = slice(g * gw, (g + 1) * gw)
            s = jnp.dot(ws_ref[g], v[rows, cols], preferred_element_type=F32) + bst_ref[:, g:g + 1]
            o_ref[0, rows, cols] = (u[rows, cols] * s).astype(o_ref.dtype)


def _gmlp(p, ln_g, ln_b, ws, bs_t):
    bsz, l_all, _ = p.shape
    nt = l_all // TILE_M
    return pl.pallas_call(
        _gmlp_kernel,
        grid=(bsz, nt),
        in_specs=[pl.BlockSpec((1, TILE_M, 2 * GMLP_W), lambda b, q: (b, q, SEG_GM // (2 * GMLP_W))),
                  _const_spec(ln_g.shape), _const_spec(ln_b.shape), _const_spec(ws.shape), _const_spec(bs_t.shape)],
        out_specs=pl.BlockSpec((1, TILE_M, GMLP_W), lambda b, q: (b, q, 0)),
        out_shape=jax.ShapeDtypeStruct((bsz, l_all, GMLP_W), BF16),
        compiler_params=_params("parallel", "parallel"),
        name="gmlp",
    )(p, ln_g, ln_b, ws, bs_t)


def _pool_kernel(p_ref, w_ref, sc_ref, o_ref, pad_ref, *, n_ctx):
    n_all = p_ref.shape[1]
    gw = LANES
    t = lax.broadcasted_iota(I32, (n_all, gw), 0)
    seq_lo = jnp.where(t < n_ctx, 0, n_ctx)
    seq_hi = jnp.where(t < n_ctx, n_ctx, n_all)
    pad_ref[0:POOL_PAD, :] = jnp.zeros((POOL_PAD, gw), F32)
    pad_ref[POOL_PAD + n_all:, :] = jnp.zeros((POOL_PAD, gw), F32)
    for i, win in enumerate(POOL_WINDOWS):
        cols = slice(i * gw, (i + 1) * gw)
        x = p_ref[0, :, cols].astype(F32)
        pad_ref[POOL_PAD:POOL_PAD + n_all, :] = x
        acc = jnp.zeros((n_all, gw), F32)
        cnt = jnp.zeros((n_all, gw), F32)
        for d in range(-(win // 2), win - win // 2):
            valid = jnp.logical_and(t + d >= seq_lo, t + d < seq_hi)
            acc = acc + jnp.where(valid, pad_ref[POOL_PAD + d:POOL_PAD + d + n_all, :], 0.0)
            cnt = cnt + valid.astype(F32)
        resid = (acc / cnt - x).astype(BF16)
        o = jnp.dot(resid, w_ref[i], preferred_element_type=F32) * sc_ref[:, cols]
        o_ref[0, :, cols] = o.astype(o_ref.dtype)


def _pool(p, pool_w, pool_scale, n_ctx):
    bsz, l_all, _ = p.shape
    width = len(POOL_WINDOWS) * LANES
    return pl.pallas_call(
        functools.partial(_pool_kernel, n_ctx=n_ctx),
        grid=(bsz,),
        in_specs=[pl.BlockSpec((1, l_all, width), lambda b: (b, 0, SEG_PL // width)),
                  _const_spec(pool_w.shape), _const_spec(pool_scale.shape)],
        out_specs=pl.BlockSpec((1, l_all, width), lambda b: (b, 0, 0)),
        out_shape=jax.ShapeDtypeStruct((bsz, l_all, width), BF16),
        scratch_shapes=[pltpu.VMEM((l_all + 2 * POOL_PAD, LANES), F32)],
        compiler_params=_params("parallel"),
        name="pool",
    )(p, pool_w, pool_scale)


def _merge_kernel(b0_ref, b1_ref, b2_ref, b3_ref, gl_ref, x_ref, g1_ref, wb_ref, wo_ref, lg_ref, lb_ref, o_ref, *,
                  alpha):
    d = x_ref.shape[2]
    merged = None
    for n, br in enumerate((b0_ref, b1_ref, b2_ref, b3_ref)):
        t = jnp.dot(br[0], wb_ref[n], preferred_element_type=F32)
        term = _sigmoid_t(gl_ref[0, :, n * d:(n + 1) * d].astype(F32)) * t
        merged = term if merged is None else merged + term
    y = jnp.dot(merged.astype(BF16), wo_ref[...], preferred_element_type=F32)
    o_ref[0] = _layernorm(alpha * x_ref[0] + g1_ref[0] * y, lg_ref[...], lb_ref[...])


def _merge(branches, gl, x, g1, w_branch, w_out, ln_g, ln_b, n_ctx_tiles, alpha):
    bsz, l_all, d = x.shape
    nt = l_all // TILE_M
    row = lambda width: pl.BlockSpec((1, TILE_M, width), lambda b, q: (b, q, 0))
    return pl.pallas_call(
        functools.partial(_merge_kernel, alpha=alpha),
        grid=(bsz, nt),
        in_specs=[row(BRANCH_W)] * N_BRANCH + [row(N_BRANCH * d), row(d), _mod_spec(n_ctx_tiles, d, "bq"),
                                               _const_spec(w_branch.shape), _const_spec(w_out.shape),
                                               _const_spec(ln_g.shape), _const_spec(ln_b.shape)],
        out_specs=row(d),
        out_shape=jax.ShapeDtypeStruct((bsz, l_all, d), F32),
        compiler_params=_params("parallel", "parallel"),
        name="merge",
    )(*branches, gl, x, g1, w_branch, w_out, ln_g, ln_b)


def _pack_words(y):
    d = y.shape[1]
    bits = pltpu.bitcast(y.astype(BF16).astype(F32), U32)
    return (bits[:, :d // 2] >> 16) | bits[:, d // 2:]


def _unpack_words(w):
    return pltpu.bitcast(w << 16, F32), pltpu.bitcast(w & jnp.uint32(0xFFFF0000), F32)


def _store_token_tiles(ref, words):
    m = words.shape[0]
    for c in range(SUBLANES):
        ref[pl.ds(c, m, stride=SUBLANES), :] = words[:, c * LANES:(c + 1) * LANES]


def _route_kernel(x_ref, sc_ref, sh_ref, rwh_ref, rwl_ref, rb_ref, hp_ref, idx_ref, wt_ref):
    h = x_ref[0] * (1.0 + sc_ref[0]) + sh_ref[0]
    h_hi = h.astype(BF16)
    h_lo = (h - h_hi.astype(F32)).astype(BF16)
    logits = (jnp.dot(h_hi, rwh_ref[...], preferred_element_type=F32)
              + jnp.dot(h_lo, rwh_ref[...], preferred_element_type=F32)
              + jnp.dot(h_hi, rwl_ref[...], preferred_element_type=F32))
    scores = _sigmoid(logits)
    biased = scores + rb_ref[...]
    lane = lax.broadcasted_iota(I32, biased.shape, 1)
    neg = jnp.float32(-jnp.inf)
    best = None
    for g in range(N_GROUPS):
        in_g = jnp.logical_and(lane >= g * EXPERTS_PER_GROUP, lane < (g + 1) * EXPERTS_PER_GROUP)
        a = jnp.where(in_g, biased, neg)
        m1 = jnp.max(a, axis=-1, keepdims=True)
        i1 = jnp.min(jnp.where(a == m1, lane, LANES), axis=-1, keepdims=True)
        a2 = jnp.where(lane == i1, neg, a)
        m2 = jnp.max(a2, axis=-1, keepdims=True)
        i2 = jnp.min(jnp.where(a2 == m2, lane, LANES), axis=-1, keepdims=True)
        gs = m1 + m2
        if best is None:
            best = (gs, i1, i2)
        else:
            take = gs > best[0]
            best = (jnp.where(take, gs, best[0]), jnp.where(take, i1, best[1]), jnp.where(take, i2, best[2]))
    _, e0, e1 = best
    w0 = jnp.sum(jnp.where(lane == e0, scores, 0.0), axis=-1, keepdims=True)
    w1 = jnp.sum(jnp.where(lane == e1, scores, 0.0), axis=-1, keepdims=True)
    tot = w0 + w1
    idx_ref[...] = jnp.where(lane == 0, e0, jnp.where(lane == 1, e1, 0))
    wt_ref[...] = jnp.where(lane == 0, w0 / tot, jnp.where(lane == 1, w1 / tot, 0.0))
    _store_token_tiles(hp_ref, _pack_words(h))


def _route(x, sc, sh, rw_hi, rw_lo, rb, n_ctx_tiles):
    bsz, l_all, d = x.shape
    nt = l_all // TILE_M
    t_all = bsz * l_all
    flat = lambda b, q: (b * nt + q, 0)
    return pl.pallas_call(
        _route_kernel,
        grid=(bsz, nt),
        in_specs=[pl.BlockSpec((1, TILE_M, d), lambda b, q: (b, q, 0)),
                  _mod_spec(n_ctx_tiles, d, "bq"), _mod_spec(n_ctx_tiles, d, "bq"),
                  _const_spec(rw_hi.shape), _const_spec(rw_lo.shape), _const_spec(rb.shape)],
        out_specs=[pl.BlockSpec((TILE_M * SUBLANES, LANES), flat), pl.BlockSpec((TILE_M, LANES), flat),
                   pl.BlockSpec((TILE_M, LANES), flat)],
        out_shape=[jax.ShapeDtypeStruct((t_all * SUBLANES, LANES), U32), jax.ShapeDtypeStruct((t_all, LANES), I32),
                   jax.ShapeDtypeStruct((t_all, LANES), F32)],
        compiler_params=_params("parallel", "parallel"),
        name="moe_route",
    )(x, sc, sh, rw_hi, rw_lo, rb)


def _route_tables(e_pairs, t_all, n_tiles):
    onehot = (e_pairs[:, None] == jnp.arange(N_EXPERTS, dtype=I32)[None, :]).astype(I32)
    csum = jnp.cumsum(onehot, axis=0)
    rank = jnp.sum((csum - onehot) * onehot, axis=1)
    counts = csum[-1]
    padded = ((counts + TILE_M - 1) // TILE_M) * TILE_M
    ends = jnp.cumsum(padded)
    pos = (ends - padded)[e_pairs] + rank
    tok = jnp.tile(jnp.arange(t_all, dtype=I32), 2)
    src_tok = jnp.zeros(((n_tiles + 1) * TILE_M,), I32).at[pos].set(tok)
    tile_start = jnp.arange(n_tiles, dtype=I32) * TILE_M
    tile_expert = jnp.minimum(jnp.sum((tile_start[:, None] >= ends[None, :]).astype(I32), axis=1), N_EXPERTS - 1)
    pos = jnp.concatenate([pos.astype(I32), jnp.zeros((TILE_M,), I32)])
    return src_tok, pos, tile_expert


def _start_row_gather(idx_ref, base, n_rows, copy_row):
    for r in range(n_rows):
        copy_row(r, idx_ref[base + r], r % 2)


def _wait_rows(like_ref, dst_ref, sem):
    pltpu.make_async_copy(like_ref.at[pl.ds(0, dst_ref.shape[0])], dst_ref, sem).wait()


def _double_buffered_step(i, n_steps, bufs, sem, src_ref, fetch, compute):
    pl.when(i == 0)(lambda: fetch(0, bufs[0], sem.at[0]))
    for parity in range(2):
        cur, nxt = bufs[parity], bufs[1 - parity]

        @pl.when(i % 2 == parity)
        def _(cur=cur, nxt=nxt, parity=parity):
            _wait_rows(src_ref, cur, sem.at[parity])
            fetch(i + 1, nxt, sem.at[1 - parity])
            compute(cur)

    last = n_steps % 2
    pl.when(i == n_steps - 1)(lambda: _wait_rows(src_ref, bufs[last], sem.at[last]))


def _expert_kernel(te_ref, tok_ref, hp_ref, hp_like_ref, w1_ref, w3_ref, w2_ref, o_ref, rows0_ref, rows1_ref, w1b_ref,
                   w3b_ref, w2b_ref, sem, *, n_tiles):
    i = pl.program_id(0)

    @pl.when(jnp.logical_or(i == 0, te_ref[i] != te_ref[jnp.maximum(i - 1, 0)]))
    def _():
        w1b_ref[...] = w1_ref[0, 0].astype(BF16)
        w3b_ref[...] = w3_ref[0, 0].astype(BF16)
        w2b_ref[...] = w2_ref[0, 0].astype(BF16)

    def fetch(tile, dst_ref, dst_sem):
        def copy_row(r, tok, queue):
            pltpu.async_copy(hp_ref.at[tok], dst_ref.at[r // SUBLANES, :, r % SUBLANES, :], dst_sem, priority=queue)

        _start_row_gather(tok_ref, tile * TILE_M, TILE_M, copy_row)

    def compute(rows_ref):
        half = w1b_ref.shape[0] // 2
        words = jnp.concatenate([rows_ref[:, c, :, :].reshape(TILE_M, LANES) for c in range(SUBLANES)], axis=1)
        lo, hi = _unpack_words(words)
        lo, hi = lo.astype(BF16), hi.astype(BF16)
        a = (jnp.dot(lo, w1b_ref[:half, :], preferred_element_type=F32)
             + jnp.dot(hi, w1b_ref[half:, :], preferred_element_type=F32))
        b = (jnp.dot(lo, w3b_ref[:half, :], preferred_element_type=F32)
             + jnp.dot(hi, w3b_ref[half:, :], preferred_element_type=F32))
        hid = (a * _sigmoid_t(a) * b).astype(BF16)
        o_ref[...] = _pack_words(jnp.dot(hid, w2b_ref[...], preferred_element_type=F32))

    _double_buffered_step(i, n_tiles, (rows0_ref, rows1_ref), sem, hp_like_ref, fetch, compute)


def _experts(hp, src_tok, tile_expert, w1, w3, w2, layer):
    n_tiles = tile_expert.shape[0]
    _, _, d, de = w1.shape
    t_all = hp.shape[0] // SUBLANES
    groups = TILE_M // SUBLANES
    rows = pltpu.VMEM((groups, d // 2 // LANES, SUBLANES, LANES), U32)
    return pl.pallas_call(
        functools.partial(_expert_kernel, n_tiles=n_tiles),
        grid_spec=pltpu.PrefetchScalarGridSpec(
            num_scalar_prefetch=2,
            grid=(n_tiles,),
            in_specs=[pl.BlockSpec(memory_space=pl.ANY), pl.BlockSpec(memory_space=pl.ANY),
                      pl.BlockSpec((1, 1, d, de), lambda i, te, tok: (layer, te[i], 0, 0)),
                      pl.BlockSpec((1, 1, d, de), lambda i, te, tok: (layer, te[i], 0, 0)),
                      pl.BlockSpec((1, 1, de, d), lambda i, te, tok: (layer, te[i], 0, 0))],
            out_specs=pl.BlockSpec((TILE_M, d // 2), lambda i, te, tok: (i, 0)),
            scratch_shapes=[rows, rows, pltpu.VMEM((d, de), BF16), pltpu.VMEM((d, de), BF16),
                            pltpu.VMEM((de, d), BF16), pltpu.SemaphoreType.DMA((2,))]),
        out_shape=jax.ShapeDtypeStruct((n_tiles * TILE_M, d // 2), U32),
        compiler_params=_params("arbitrary"),
        name="moe_experts",
    )(tile_expert, src_tok, hp.reshape(t_all, SUBLANES, LANES),
      hp.reshape(t_all // SUBLANES, SUBLANES, SUBLANES, LANES), w1, w3, w2)


def _combine_kernel(pos_ref, ys_ref, wt_ref, x_ref, g2_ref, lg_ref, lb_ref, o_ref, rows0_ref, rows1_ref, sem, *,
                    alpha, t_all, n_tiles):
    def fetch(tile, dst_ref, dst_sem):
        for k in range(2):
            def copy_row(r, row, queue, k=k):
                pltpu.async_copy(ys_ref.at[pl.ds(row, 1)], dst_ref.at[pl.ds(k * TILE_M + r, 1)], dst_sem, priority=queue)

            _start_row_gather(pos_ref, k * t_all + tile * TILE_M, TILE_M, copy_row)

    def compute(rows_ref):
        w = wt_ref[...]
        lo0, hi0 = _unpack_words(rows_ref[:TILE_M, :])
        lo1, hi1 = _unpack_words(rows_ref[TILE_M:, :])
        w0, w1 = w[:, 0:1], w[:, 1:2]
        y = jnp.concatenate([w0 * lo0 + w1 * lo1, w0 * hi0 + w1 * hi1], axis=1)
        o_ref[0] = _layernorm(alpha * x_ref[0] + g2_ref[0] * y, lg_ref[...], lb_ref[...])

    _double_buffered_step(pl.program_id(0), n_tiles, (rows0_ref, rows1_ref), sem, ys_ref, fetch, compute)


def _combine(ys, pos, wt, x, g2, ln_g, ln_b, n_ctx_tiles, alpha, latent_only):
    bsz, l_all, d = x.shape
    nt = l_all // TILE_M
    rows = pltpu.VMEM((2 * TILE_M, d // 2), U32)
    if latent_only:
        out_rows = l_all - n_ctx_tiles * TILE_M
        out_map = lambda i, pos: (i // nt, jnp.maximum(i % nt - n_ctx_tiles, 0), 0)
    else:
        out_rows = l_all
        out_map = lambda i, pos: (i // nt, i % nt, 0)
    return pl.pallas_call(
        functools.partial(_combine_kernel, alpha=alpha, t_all=bsz * l_all, n_tiles=bsz * nt),
        grid_spec=pltpu.PrefetchScalarGridSpec(
            num_scalar_prefetch=1,
            grid=(bsz * nt,),
            in_specs=[pl.BlockSpec(memory_space=pl.ANY),
                      pl.BlockSpec((TILE_M, LANES), lambda i, pos: (i, 0)),
                      pl.BlockSpec((1, TILE_M, d), lambda i, pos: (i // nt, i % nt, 0)),
                      pl.BlockSpec((1, 1, d), lambda i, pos: (2 * (i // nt) + (i % nt >= n_ctx_tiles).astype(I32), 0, 0)),
                      pl.BlockSpec(ln_g.shape, lambda i, pos: (0, 0), pipeline_mode=pl.Buffered(1)),
                      pl.BlockSpec(ln_b.shape, lambda i, pos: (0, 0), pipeline_mode=pl.Buffered(1))],
            out_specs=pl.BlockSpec((1, TILE_M, d), out_map),
            scratch_shapes=[rows, rows, pltpu.SemaphoreType.DMA((2,))]),
        out_shape=jax.ShapeDtypeStruct((bsz, out_rows, d), F32),
        compiler_params=_params("arbitrary"),
        name="moe_combine",
    )(pos, ys, wt, x, g2, ln_g, ln_b)


def _rope_tables(n_ctx, n_lat):
    rows = n_lat // GRID_W
    row = jnp.repeat(jnp.arange(rows, dtype=F32), GRID_W)
    col = jnp.tile(jnp.arange(GRID_W, dtype=F32), rows)
    n_freq = ROPE_DIM // 4
    inv = ROPE_BASE ** (-jnp.arange(n_freq, dtype=F32) / n_freq)
    ang = jnp.concatenate([row[:, None] * inv, col[:, None] * inv], axis=-1)
    cos = jnp.concatenate([jnp.ones((n_ctx, ROPE_DIM // 2), F32), jnp.cos(ang)], axis=0)
    sin = jnp.concatenate([jnp.zeros((n_ctx, ROPE_DIM // 2), F32), jnp.sin(ang)], axis=0)
    reps = LANES // ROPE_DIM
    return jnp.tile(jnp.concatenate([cos, cos], axis=1), (1, reps)), jnp.tile(jnp.concatenate([-sin, sin], axis=1), (1, reps))


def _split_w_in(w):
    d = w.shape[0]
    o = 0
    segs = {}
    for name, width in (("ckv", MLA_KV_RANK), ("kpe", MLA_ROPE), ("dk", 512), ("dv", 512), ("cq", MLA_Q_RANK),
                        ("dq", 512), ("gm", 2 * GMLP_W), ("pl", 512), ("gate", N_BRANCH * d)):
        segs[name] = w[:, o:o + width]
        o += width
    small = jnp.concatenate([segs["dk"], segs["dv"], segs["dq"], segs["pl"], segs["gm"], segs["ckv"], segs["kpe"],
                             jnp.zeros((d, LANES - MLA_ROPE), w.dtype), segs["cq"]], axis=1)
    return small.astype(BF16), segs["gate"].astype(BF16)


def _forward(x, c, ctx, c_ctx, ada_w, ada_b, w_in, mla_q_norm, mla_kv_norm, mla_w_uq, mla_w_ukv, diff_lambda,
             diff_subln, gmlp_ln_g, gmlp_ln_b, gmlp_ws, gmlp_bs, pool_w, pool_scale, w_branch, w_out, ln1_g, ln1_b,
             ln2_g, ln2_b, router_w, router_bias, moe_w1, moe_w3, moe_w2):
    bsz, n_lat, d = x.shape
    n_ctx = ctx.shape[1]
    depth = w_in.shape[0]
    l_all = n_ctx + n_lat
    t_all = bsz * l_all
    n_ctx_tiles = n_ctx // TILE_M
    alpha = (2 * depth) ** 0.25
    n_row_tiles = (2 * t_all) // TILE_M + N_EXPERTS

    xs = jnp.concatenate([ctx, x], axis=1)
    cos_t, sin_t = _rope_tables(n_ctx, n_lat)

    c_rows = ((bsz + 1 + SUBLANES - 1) // SUBLANES) * SUBLANES
    c_all = jnp.zeros((c_rows, d), F32).at[:bsz].set(c).at[bsz].set(c_ctx)
    mod = _ada(c_all, ada_w, ada_b)
    pick = jnp.stack([jnp.full((bsz,), bsz, I32), jnp.arange(bsz, dtype=I32)], axis=1).reshape(-1)
    mod = mod[:, pick].reshape(depth, 2 * bsz, 1, 6, d)

    rw = jnp.zeros((d, LANES), F32).at[:, :N_EXPERTS].set(router_w)
    rw_hi = rw.astype(BF16)
    rw_lo = (rw - rw_hi.astype(F32)).astype(BF16)
    rb = jnp.zeros((1, LANES), F32).at[0, :N_EXPERTS].set(router_bias)
    w_in16 = w_in.astype(BF16)

    for l in range(depth):
        sh1, sc1, g1, sh2, sc2, g2 = (mod[l, :, :, k] for k in range(6))
        lam_init = 0.8 - 0.6 * math.exp(-0.3 * l)
        w_small, w_gate = _split_w_in(w_in16[l])
        w_ukv = mla_w_ukv[l].reshape(MLA_KV_RANK, MLA_HEADS, 2, MLA_NOPE).transpose(0, 2, 1, 3).reshape(MLA_KV_RANK, -1)
        w_uq = jnp.pad(mla_w_uq[l].reshape(MLA_Q_RANK, MLA_HEADS, MLA_NOPE + MLA_ROPE),
                       ((0, 0), (0, 0), (0, MLA_HEAD_PAD - MLA_NOPE - MLA_ROPE))).reshape(MLA_Q_RANK, -1)

        p = _inproj(xs, sc1, sh1, w_small, 1280, n_ctx_tiles, "inproj_small")
        gl = _inproj(xs, sc1, sh1, w_gate, 2048, n_ctx_tiles, "inproj_gate")
        mq, mk, mv, dqr, dkr, dvx = _prep(p, cos_t, sin_t, mla_kv_norm[l][None], mla_q_norm[l][None],
                                          w_ukv.astype(BF16), w_uq.astype(BF16))
        mla_out = _mla_attention(mq, mk, mv, n_ctx)
        d_out = _diff_attention(dqr, dkr, dvx, diff_lambda[l], diff_subln[l][None], n_ctx, lam_init)
        g_out = _gmlp(p, gmlp_ln_g[l][None], gmlp_ln_b[l][None], gmlp_ws[l].astype(BF16), gmlp_bs[l].T)
        p_out = _pool(p, pool_w[l].astype(BF16), pool_scale[l][None], n_ctx)
        xs = _merge((mla_out, d_out, g_out, p_out), gl, xs, g1, w_branch[l].astype(BF16), w_out[l].astype(BF16),
                    ln1_g[l][None], ln1_b[l][None], n_ctx_tiles, alpha)

        hp, idx, wt = _route(xs, sc2, sh2, rw_hi, rw_lo, rb, n_ctx_tiles)
        e_pairs = idx[:, :2].T.reshape(-1)
        src_tok, pos, tile_expert = _route_tables(e_pairs, t_all, n_row_tiles)
        ys = _experts(hp, src_tok, tile_expert, moe_w1, moe_w3, moe_w2, l)
        xs = _combine(ys, pos, wt, xs, g2, ln2_g[l][None], ln2_b[l][None], n_ctx_tiles, alpha, l == depth - 1)
    return xs


_forward_jit = jax.jit(_forward)


def kernel(x, c, ctx, c_ctx, ada_w, ada_b, w_in, mla_q_norm, mla_kv_norm, mla_w_uq, mla_w_ukv, diff_lambda, diff_subln, gmlp_ln_g, gmlp_ln_b, gmlp_ws, gmlp_bs, pool_w, pool_scale, w_branch, w_out, ln1_g, ln1_b, ln2_g, ln2_b, router_w, router_bias, moe_w1, moe_w3, moe_w2):
    return _forward_jit(x, c, ctx, c_ctx, ada_w, ada_b, w_in, mla_q_norm, mla_kv_norm, mla_w_uq, mla_w_ukv,
                        diff_lambda, diff_subln, gmlp_ln_g, gmlp_ln_b, gmlp_ws, gmlp_bs, pool_w, pool_scale,
                        w_branch, w_out, ln1_g, ln1_b, ln2_g, ln2_b, router_w, router_bias, moe_w1, moe_w3, moe_w2)
```

```python
import functools
import math

import jax
import jax.numpy as jnp
from jax import lax
from jax.experimental import pallas as pl
from jax.experimental.pallas import tpu as pltpu

F32 = jnp.float32
BF16 = jnp.bfloat16
U32 = jnp.uint32
I32 = jnp.int32

LANES = 128
SUBLANES = 8
TILE_M = 256
VMEM_LIMIT = 56 << 20

GRID_W = 64
ROPE_DIM = 64
ROPE_BASE = 10000.0
MLA_HEADS = 4
MLA_NOPE = 128
MLA_ROPE = 64
MLA_V = 128
MLA_Q_RANK = 384
MLA_KV_RANK = 256
MLA_HEAD_PAD = 256
DIFF_HEADS = 4
DIFF_QK = 64
DIFF_V = 128
CHUNK = 128
GMLP_GROUPS = 4
GMLP_W = 512
POOL_WINDOWS = (2, 4, 8, 16)
POOL_PAD = 16
N_BRANCH = 4
BRANCH_W = 512
N_EXPERTS = 32
N_GROUPS = 4
EXPERTS_PER_GROUP = 8
D_EXPERT = 512
LN_EPS = 1e-5
RMS_EPS = 1e-6

SEG_DK, SEG_DV, SEG_DQ, SEG_PL, SEG_GM, SEG_CKV, SEG_KPE, SEG_CQ = 0, 512, 1024, 1536, 2048, 3072, 3328, 3456
N_SMALL = 3840


def _params(*sem):
    return pltpu.CompilerParams(dimension_semantics=sem, vmem_limit_bytes=VMEM_LIMIT)


def _const_spec(shape):
    nd = len(shape)
    return pl.BlockSpec(shape, lambda *_: (0,) * nd, pipeline_mode=pl.Buffered(1))


def _sigmoid(x):
    return 1.0 / (1.0 + jnp.exp(-x))


def _sigmoid_t(x):
    return 0.5 * jnp.tanh(0.5 * x) + 0.5


def _layernorm(x, g, b):
    mu = jnp.mean(x, axis=-1, keepdims=True)
    xc = x - mu
    var = jnp.mean(xc * xc, axis=-1, keepdims=True)
    return xc * lax.rsqrt(var + LN_EPS) * g + b


def _rmsnorm(x, g):
    return x * lax.rsqrt(jnp.mean(x * x, axis=-1, keepdims=True) + RMS_EPS) * g


def _rope(x, cos_t, sin_t):
    lane = lax.broadcasted_iota(I32, (x.shape[0], LANES), 1)
    first_half = (lane % ROPE_DIM) < (ROPE_DIM // 2)
    outs = []
    for j in range(x.shape[1] // LANES):
        xc = x[:, j * LANES:(j + 1) * LANES]
        partner = jnp.where(first_half, pltpu.roll(xc, LANES - ROPE_DIM // 2, 1), pltpu.roll(xc, ROPE_DIM // 2, 1))
        outs.append(xc * cos_t + partner * sin_t)
    return outs[0] if len(outs) == 1 else jnp.concatenate(outs, axis=1)


def _ada_kernel(c_ref, w_ref, b_ref, o_ref):
    c = c_ref[...]
    s = (c * _sigmoid(c)).astype(BF16)
    o_ref[0] = jnp.dot(s, w_ref[0].astype(BF16), preferred_element_type=F32) + b_ref[0]


def _ada(c_all, ada_w, ada_b):
    depth, d, n6 = ada_w.shape
    tn = 1024
    rows = c_all.shape[0]
    return pl.pallas_call(
        _ada_kernel,
        grid=(depth, n6 // tn),
        in_specs=[pl.BlockSpec((rows, d), lambda l, j: (0, 0)),
                  pl.BlockSpec((1, d, tn), lambda l, j: (l, 0, j)),
                  pl.BlockSpec((1, 1, tn), lambda l, j: (l, 0, j))],
        out_specs=pl.BlockSpec((1, rows, tn), lambda l, j: (l, 0, j)),
        out_shape=jax.ShapeDtypeStruct((depth, rows, n6), F32),
        compiler_params=_params("parallel", "parallel"),
        name="ada",
    )(c_all, ada_w, ada_b.reshape(depth, 1, n6))


def _inproj_kernel(x_ref, sc_ref, sh_ref, w_ref, o_ref):
    h = (x_ref[0] * (1.0 + sc_ref[0]) + sh_ref[0]).astype(BF16)
    o_ref[0] = jnp.dot(h, w_ref[...], preferred_element_type=F32).astype(o_ref.dtype)


def _mod_spec(n_ctx_tiles, d, order):
    if order == "jbq":
        return pl.BlockSpec((1, 1, d), lambda j, b, q: (2 * b + (q >= n_ctx_tiles).astype(I32), 0, 0))
    return pl.BlockSpec((1, 1, d), lambda b, q: (2 * b + (q >= n_ctx_tiles).astype(I32), 0, 0))


def _inproj(x, sc, sh, w, tn, n_ctx_tiles, name):
    bsz, l_all, d = x.shape
    n = w.shape[1]
    nt = l_all // TILE_M
    return pl.pallas_call(
        _inproj_kernel,
        grid=(n // tn, bsz, nt),
        in_specs=[pl.BlockSpec((1, TILE_M, d), lambda j, b, q: (b, q, 0)),
                  _mod_spec(n_ctx_tiles, d, "jbq"), _mod_spec(n_ctx_tiles, d, "jbq"),
                  pl.BlockSpec((d, tn), lambda j, b, q: (0, j))],
        out_specs=pl.BlockSpec((1, TILE_M, tn), lambda j, b, q: (b, q, j)),
        out_shape=jax.ShapeDtypeStruct((bsz, l_all, n), BF16),
        compiler_params=_params("parallel", "parallel", "parallel"),
        name=name,
    )(x, sc, sh, w)


def _prep_kernel(ckv_ref, kpe_ref, cq_ref, dq_ref, dk_ref, dv_ref, cos_ref, sin_ref, gkv_ref, gq_ref, wukv_ref, wuq_ref,
                 mq_ref, mk_ref, mv_ref, dqr_ref, dkr_ref, dvx_ref, *, mla_scale, diff_scale):
    cos_t = cos_ref[...]
    sin_t = sin_ref[...]
    ckv = _rmsnorm(ckv_ref[0].astype(F32), gkv_ref[...]).astype(BF16)
    kv = jnp.dot(ckv, wukv_ref[...], preferred_element_type=F32)
    kpe = _rope(kpe_ref[0].astype(F32), cos_t, sin_t).astype(BF16)
    lane = lax.broadcasted_iota(I32, (kv.shape[0], LANES), 1)
    ones_col = jnp.where(lane == 0, 1.0, 0.0).astype(BF16)
    for h in range(MLA_HEADS):
        base = h * MLA_HEAD_PAD
        mk_ref[0, :, base:base + MLA_NOPE] = kv[:, h * MLA_NOPE:(h + 1) * MLA_NOPE].astype(BF16)
        mk_ref[0, :, base + MLA_NOPE:base + MLA_HEAD_PAD] = kpe
        vcol = MLA_HEADS * MLA_NOPE + h * MLA_V
        mv_ref[0, :, 2 * h * LANES:(2 * h + 1) * LANES] = kv[:, vcol:vcol + MLA_V].astype(BF16)
        mv_ref[0, :, (2 * h + 1) * LANES:(2 * h + 2) * LANES] = ones_col
        dvx_ref[0, :, 2 * h * LANES:(2 * h + 1) * LANES] = dv_ref[0, :, h * DIFF_V:(h + 1) * DIFF_V]
        dvx_ref[0, :, (2 * h + 1) * LANES:(2 * h + 2) * LANES] = ones_col
    cq = _rmsnorm(cq_ref[0].astype(F32), gq_ref[...]).astype(BF16)
    q = jnp.dot(cq, wuq_ref[...], preferred_element_type=F32)
    for h in range(MLA_HEADS):
        base = h * MLA_HEAD_PAD
        mq_ref[0, :, base:base + MLA_NOPE] = (q[:, base:base + MLA_NOPE] * mla_scale).astype(BF16)
        qpe = _rope(q[:, base + MLA_NOPE:base + MLA_HEAD_PAD], cos_t, sin_t)
        mq_ref[0, :, base + MLA_NOPE:base + MLA_HEAD_PAD] = (qpe * mla_scale).astype(BF16)
    dqr_ref[0] = (_rope(dq_ref[0].astype(F32), cos_t, sin_t) * diff_scale).astype(BF16)
    dkr_ref[0] = _rope(dk_ref[0].astype(F32), cos_t, sin_t).astype(BF16)


def _prep(p, cos_t, sin_t, g_kv, g_q, w_ukv, w_uq):
    bsz, l_all, _ = p.shape
    nt = l_all // TILE_M

    def seg(width, offset):
        return pl.BlockSpec((1, TILE_M, width), lambda b, q: (b, q, offset // width))

    def out(width):
        return pl.BlockSpec((1, TILE_M, width), lambda b, q: (b, q, 0))

    log2e = math.log2(math.e)
    kern = functools.partial(_prep_kernel, mla_scale=(MLA_NOPE + MLA_ROPE) ** -0.5 * log2e,
                             diff_scale=DIFF_QK ** -0.5 * log2e)
    vx = MLA_HEADS * 2 * LANES
    widths = (MLA_HEADS * MLA_HEAD_PAD, MLA_HEADS * MLA_HEAD_PAD, vx, 512, 512, vx)
    return pl.pallas_call(
        kern,
        grid=(bsz, nt),
        in_specs=[seg(MLA_KV_RANK, SEG_CKV), seg(LANES, SEG_KPE), seg(MLA_Q_RANK, SEG_CQ), seg(512, SEG_DQ),
                  seg(512, SEG_DK), seg(512, SEG_DV),
                  pl.BlockSpec((TILE_M, LANES), lambda b, q: (q, 0)), pl.BlockSpec((TILE_M, LANES), lambda b, q: (q, 0)),
                  _const_spec(g_kv.shape), _const_spec(g_q.shape), _const_spec(w_ukv.shape), _const_spec(w_uq.shape)],
        out_specs=[out(w) for w in widths],
        out_shape=[jax.ShapeDtypeStruct((bsz, l_all, w), BF16) for w in widths],
        compiler_params=_params("parallel", "parallel"),
        name="attn_prep",
    )(p, p, p, p, p, p, cos_t, sin_t, g_kv, g_q, w_ukv, w_uq)


KEY_CHUNK = 2048
MLA_HEADS_PER_STEP = 2
ROW_BUFFERS = 3


def _key_chunks(n_ctx, n_all):
    step = math.gcd(n_all - n_ctx, KEY_CHUNK)
    return [(0, n_ctx)] + [(lo, step) for lo in range(n_ctx, n_all, step)]


def _softmax_pv(streams, k_ref, v_ref, chunks):
    def scores(stream, chunk):
        q, kc, _ = stream
        lo, n = chunk
        return lax.dot_general(q, k_ref[0, lo:lo + n, kc], (((1,), (1,)), ((), ())), preferred_element_type=F32)

    m = [None] * len(streams)
    acc = [None] * len(streams)
    s_next = [scores(st, chunks[0]) for st in streams]
    for c, (lo, n) in enumerate(chunks):
        for i, st in enumerate(streams):
            s = s_next[i]
            if c + 1 < len(chunks):
                s_next[i] = scores(st, chunks[c + 1])
            m_new = jnp.max(s, axis=-1, keepdims=True)
            if m[i] is not None:
                m_new = jnp.maximum(m[i], m_new)
            pv = jnp.dot(jnp.exp2(s - m_new).astype(BF16), v_ref[0, lo:lo + n, st[2]], preferred_element_type=F32)
            acc[i] = pv if acc[i] is None else acc[i] * jnp.exp2(m[i] - m_new) + pv
            m[i] = m_new
    return acc


def _normalised(acc, width):
    return acc[:, :width] / acc[:, width:width + 1]


def _mla_kernel(q_ref, k_ref, v_ref, o_ref, *, n_ctx):
    heads = q_ref.shape[2] // MLA_HEAD_PAD
    cols = [slice(h * MLA_HEAD_PAD, (h + 1) * MLA_HEAD_PAD) for h in range(heads)]
    streams = [(q_ref[0, :, c], c, c) for c in cols]
    chunks = _key_chunks(n_ctx, k_ref.shape[1])

    def run(chunks):
        for h, acc in enumerate(_softmax_pv(streams, k_ref, v_ref, chunks)):
            o_ref[0, :, h * MLA_V:(h + 1) * MLA_V] = _normalised(acc, MLA_V).astype(o_ref.dtype)

    is_ctx = pl.program_id(2) < n_ctx // TILE_M
    pl.when(is_ctx)(lambda: run([(0, n_ctx)]))
    pl.when(jnp.logical_not(is_ctx))(lambda: run(chunks))


def _mla_attention(mq, mk, mv, n_ctx):
    bsz, l_all, _ = mq.shape
    nt = l_all // TILE_M
    hw = MLA_HEADS_PER_STEP * MLA_HEAD_PAD
    return pl.pallas_call(
        functools.partial(_mla_kernel, n_ctx=n_ctx),
        grid=(bsz, MLA_HEADS // MLA_HEADS_PER_STEP, nt),
        in_specs=[pl.BlockSpec((1, TILE_M, hw), lambda b, h, q: (b, q, h)),
                  pl.BlockSpec((1, l_all, hw), lambda b, h, q: (b, 0, h)),
                  pl.BlockSpec((1, l_all, hw), lambda b, h, q: (b, 0, h))],
        out_specs=pl.BlockSpec((1, TILE_M, MLA_HEADS_PER_STEP * MLA_V), lambda b, h, q: (b, q, h)),
        out_shape=jax.ShapeDtypeStruct((bsz, l_all, MLA_HEADS * MLA_V), BF16),
        compiler_params=_params("parallel", "parallel", "parallel"),
        name="mla_attn",
    )(mq, mk, mv)


def _diff_kernel(q_ref, k_ref, v_ref, lam_ref, g_ref, o_ref, *, n_ctx, lam_init):
    lv = lam_ref[...]
    lam = (jnp.exp(jnp.sum(lv[0:1] * lv[1:2], axis=-1, keepdims=True))
           - jnp.exp(jnp.sum(lv[2:3] * lv[3:4], axis=-1, keepdims=True)) + lam_init)
    q = q_ref[0]
    lane = lax.broadcasted_iota(I32, q.shape, 1)
    q12 = jnp.concatenate([jnp.where(lane < DIFF_QK, q, jnp.zeros_like(q)),
                           jnp.where(lane >= DIFF_QK, q, jnp.zeros_like(q))], axis=0)
    chunks = _key_chunks(n_ctx, k_ref.shape[1])
    tq = q.shape[0]

    def run(chunks):
        acc, = _softmax_pv([(q12, slice(None), slice(None))], k_ref, v_ref, chunks)
        o = _normalised(acc[:tq], DIFF_V) - lam * _normalised(acc[tq:], DIFF_V)
        o_ref[0] = (_rmsnorm(o, g_ref[...]) * (1.0 - lam_init)).astype(o_ref.dtype)

    is_ctx = pl.program_id(2) < n_ctx // TILE_M
    pl.when(is_ctx)(lambda: run([(0, n_ctx)]))
    pl.when(jnp.logical_not(is_ctx))(lambda: run(chunks))


def _diff_attention(dqr, dkr, dvx, lam_vec, g_sub, n_ctx, lam_init):
    bsz, l_all, _ = dqr.shape
    nt = l_all // TILE_M
    w = 2 * DIFF_QK
    return pl.pallas_call(
        functools.partial(_diff_kernel, n_ctx=n_ctx, lam_init=lam_init),
        grid=(bsz, DIFF_HEADS, nt),
        in_specs=[pl.BlockSpec((1, TILE_M, w), lambda b, h, q: (b, q, h)),
                  pl.BlockSpec((1, l_all, w), lambda b, h, q: (b, 0, h)),
                  pl.BlockSpec((1, l_all, 2 * LANES), lambda b, h, q: (b, 0, h)),
                  _const_spec(lam_vec.shape), _const_spec(g_sub.shape)],
        out_specs=pl.BlockSpec((1, TILE_M, DIFF_V), lambda b, h, q: (b, q, h)),
        out_shape=jax.ShapeDtypeStruct((bsz, l_all, DIFF_HEADS * DIFF_V), BF16),
        compiler_params=_params("parallel", "parallel", "parallel"),
        name="diff_attn",
    )(dqr, dkr, dvx, lam_vec, g_sub)


def _gelu_tanh(x):
    return 0.5 * x * (1.0 + jnp.tanh(math.sqrt(2.0 / math.pi) * (x + 0.044715 * (x * x * x))))


def _gmlp_kernel(z_ref, g_ref, b_ref, ws_ref, bst_ref, o_ref):
    z = _gelu_tanh(z_ref[0].astype(F32))
    u = z[:, :GMLP_W]
    v = _layernorm(z[:, GMLP_W:], g_ref[...], b_ref[...]).astype(BF16)
    gw = GMLP_W // GMLP_GROUPS
    for c in range(z.shape[0] // CHUNK):
        rows = slice(c * CHUNK, (c + 1) * CHUNK)
        for g in range(GMLP_GROUPS):
            cols = slice(g * gw, (g + 1) * gw)
            s = jnp.dot(ws_ref[g], v[rows, cols], preferred_element_type=F32) + bst_ref[:, g:g + 1]
            o_ref[0, rows, cols] = (u[rows, cols] * s).astype(o_ref.dtype)


def _gmlp(p, ln_g, ln_b, ws, bs_t):
    bsz, l_all, _ = p.shape
    nt = l_all // TILE_M
    return pl.pallas_call(
        _gmlp_kernel,
        grid=(bsz, nt),
        in_specs=[pl.BlockSpec((1, TILE_M, 2 * GMLP_W), lambda b, q: (b, q, SEG_GM // (2 * GMLP_W))),
                  _const_spec(ln_g.shape), _const_spec(ln_b.shape), _const_spec(ws.shape), _const_spec(bs_t.shape)],
        out_specs=pl.BlockSpec((1, TILE_M, GMLP_W), lambda b, q: (b, q, 0)),
        out_shape=jax.ShapeDtypeStruct((bsz, l_all, GMLP_W), BF16),
        compiler_params=_params("parallel", "parallel"),
        name="gmlp",
    )(p, ln_g, ln_b, ws, bs_t)


def _pool_kernel(p_ref, w_ref, sc_ref, o_ref, pad_ref, *, n_ctx):
    n_all = p_ref.shape[1]
    gw = LANES
    t = lax.broadcasted_iota(I32, (n_all, gw), 0)
    seq_lo = jnp.where(t < n_ctx, 0, n_ctx)
    seq_hi = jnp.where(t < n_ctx, n_ctx, n_all)
    pad_ref[0:POOL_PAD, :] = jnp.zeros((POOL_PAD, gw), F32)
    pad_ref[POOL_PAD + n_all:, :] = jnp.zeros((POOL_PAD, gw), F32)
    for i, win in enumerate(POOL_WINDOWS):
        cols = slice(i * gw, (i + 1) * gw)
        x = p_ref[0, :, cols].astype(F32)
        pad_ref[POOL_PAD:POOL_PAD + n_all, :] = x
        acc = jnp.zeros((n_all, gw), F32)
        cnt = jnp.zeros((n_all, gw), F32)
        for d in range(-(win // 2), win - win // 2):
            valid = jnp.logical_and(t + d >= seq_lo, t + d < seq_hi)
            acc = acc + jnp.where(valid, pad_ref[POOL_PAD + d:POOL_PAD + d + n_all, :], 0.0)
            cnt = cnt + valid.astype(F32)
        resid = (acc / cnt - x).astype(BF16)
        o = jnp.dot(resid, w_ref[i], preferred_element_type=F32) * sc_ref[:, cols]
        o_ref[0, :, cols] = o.astype(o_ref.dtype)


def _pool(p, pool_w, pool_scale, n_ctx):
    bsz, l_all, _ = p.shape
    width = len(POOL_WINDOWS) * LANES
    return pl.pallas_call(
        functools.partial(_pool_kernel, n_ctx=n_ctx),
        grid=(bsz,),
        in_specs=[pl.BlockSpec((1, l_all, width), lambda b: (b, 0, SEG_PL // width)),
                  _const_spec(pool_w.shape), _const_spec(pool_scale.shape)],
        out_specs=pl.BlockSpec((1, l_all, width), lambda b: (b, 0, 0)),
        out_shape=jax.ShapeDtypeStruct((bsz, l_all, width), BF16),
        scratch_shapes=[pltpu.VMEM((l_all + 2 * POOL_PAD, LANES), F32)],
        compiler_params=_params("parallel"),
        name="pool",
    )(p, pool_w, pool_scale)


def _merge_kernel(b0_ref, b1_ref, b2_ref, b3_ref, gl_ref, x_ref, g1_ref, wb_ref, wo_ref, lg_ref, lb_ref, o_ref, *,
                  alpha):
    d = x_ref.shape[2]
    merged = None
    for n, br in enumerate((b0_ref, b1_ref, b2_ref, b3_ref)):
        t = jnp.dot(br[0], wb_ref[n], preferred_element_type=F32)
        term = _sigmoid_t(gl_ref[0, :, n * d:(n + 1) * d].astype(F32)) * t
        merged = term if merged is None else merged + term
    y = jnp.dot(merged.astype(BF16), wo_ref[...], preferred_element_type=F32)
    o_ref[0] = _layernorm(alpha * x_ref[0] + g1_ref[0] * y, lg_ref[...], lb_ref[...])


def _merge(branches, gl, x, g1, w_branch, w_out, ln_g, ln_b, n_ctx_tiles, alpha):
    bsz, l_all, d = x.shape
    nt = l_all // TILE_M
    row = lambda width: pl.BlockSpec((1, TILE_M, width), lambda b, q: (b, q, 0))
    return pl.pallas_call(
        functools.partial(_merge_kernel, alpha=alpha),
        grid=(bsz, nt),
        in_specs=[row(BRANCH_W)] * N_BRANCH + [row(N_BRANCH * d), row(d), _mod_spec(n_ctx_tiles, d, "bq"),
                                               _const_spec(w_branch.shape), _const_spec(w_out.shape),
                                               _const_spec(ln_g.shape), _const_spec(ln_b.shape)],
        out_specs=row(d),
        out_shape=jax.ShapeDtypeStruct((bsz, l_all, d), F32),
        compiler_params=_params("parallel", "parallel"),
        name="merge",
    )(*branches, gl, x, g1, w_branch, w_out, ln_g, ln_b)


def _pack_words(y):
    d = y.shape[1]
    bits = pltpu.bitcast(y.astype(BF16).astype(F32), U32)
    return (bits[:, :d // 2] >> 16) | bits[:, d // 2:]


def _unpack_words(w):
    return pltpu.bitcast(w << 16, F32), pltpu.bitcast(w & jnp.uint32(0xFFFF0000), F32)


def _store_token_tiles(ref, words):
    m = words.shape[0]
    for c in range(SUBLANES):
        ref[pl.ds(c, m, stride=SUBLANES), :] = words[:, c * LANES:(c + 1) * LANES]


def _route_kernel(x_ref, sc_ref, sh_ref, rwh_ref, rwl_ref, rb_ref, hp_ref, meta_ref):
    h = x_ref[0] * (1.0 + sc_ref[0]) + sh_ref[0]
    h_hi = h.astype(BF16)
    h_lo = (h - h_hi.astype(F32)).astype(BF16)
    nt = (((1,), (1,)), ((), ()))
    logits = (lax.dot_general(rwh_ref[...], h_hi, nt, preferred_element_type=F32)
              + lax.dot_general(rwh_ref[...], h_lo, nt, preferred_element_type=F32)
              + lax.dot_general(rwl_ref[...], h_hi, nt, preferred_element_type=F32))
    scores = _sigmoid(logits)
    biased = scores + rb_ref[...]
    m = h.shape[0]
    neg = jnp.float32(-jnp.inf)
    far = jnp.float32(N_EXPERTS)
    best = None
    for g in range(N_GROUPS):
        a = biased[g * EXPERTS_PER_GROUP:(g + 1) * EXPERTS_PER_GROUP, :]
        ex = (lax.broadcasted_iota(I32, a.shape, 0) + g * EXPERTS_PER_GROUP).astype(F32)
        m1 = jnp.max(a, axis=0, keepdims=True)
        i1 = jnp.min(jnp.where(a == m1, ex, far), axis=0, keepdims=True)
        a2 = jnp.where(ex == i1, neg, a)
        m2 = jnp.max(a2, axis=0, keepdims=True)
        i2 = jnp.min(jnp.where(a2 == m2, ex, far), axis=0, keepdims=True)
        gs = m1 + m2
        if best is None:
            best = (gs, i1, i2)
        else:
            take = gs > best[0]
            best = (jnp.where(take, gs, best[0]), jnp.where(take, i1, best[1]), jnp.where(take, i2, best[2]))
    _, e0, e1 = best
    ex_all = lax.broadcasted_iota(I32, scores.shape, 0).astype(F32)
    w0 = jnp.sum(jnp.where(ex_all == e0, scores, 0.0), axis=0, keepdims=True)
    w1 = jnp.sum(jnp.where(ex_all == e1, scores, 0.0), axis=0, keepdims=True)
    tot = w0 + w1
    rows = jnp.concatenate([e0, e1, w0 / tot, w1 / tot, jnp.zeros((LANES - 4, m), F32)], axis=0)
    meta_ref[...] = rows.T
    _store_token_tiles(hp_ref, _pack_words(h))


def _route(x, sc, sh, rw_hi, rw_lo, rb, n_ctx_tiles):
    bsz, l_all, d = x.shape
    nt = l_all // TILE_M
    t_all = bsz * l_all
    flat = lambda b, q: (b * nt + q, 0)
    return pl.pallas_call(
        _route_kernel,
        grid=(bsz, nt),
        in_specs=[pl.BlockSpec((1, TILE_M, d), lambda b, q: (b, q, 0)),
                  _mod_spec(n_ctx_tiles, d, "bq"), _mod_spec(n_ctx_tiles, d, "bq"),
                  _const_spec(rw_hi.shape), _const_spec(rw_lo.shape), _const_spec(rb.shape)],
        out_specs=[pl.BlockSpec((TILE_M * SUBLANES, LANES), flat), pl.BlockSpec((TILE_M, LANES), flat)],
        out_shape=[jax.ShapeDtypeStruct((t_all * SUBLANES, LANES), U32), jax.ShapeDtypeStruct((t_all, LANES), F32)],
        compiler_params=_params("parallel", "parallel"),
        name="moe_route",
    )(x, sc, sh, rw_hi, rw_lo, rb)


def _route_tables(e_pairs, t_all, n_tiles):
    onehot = (e_pairs[:, None] == jnp.arange(N_EXPERTS, dtype=I32)[None, :]).astype(I32)
    csum = jnp.cumsum(onehot, axis=0)
    rank = jnp.sum((csum - onehot) * onehot, axis=1)
    counts = csum[-1]
    padded = ((counts + TILE_M - 1) // TILE_M) * TILE_M
    ends = jnp.cumsum(padded)
    pos = (ends - padded)[e_pairs] + rank
    tok = jnp.tile(jnp.arange(t_all, dtype=I32), 2)
    src_tok = jnp.zeros(((n_tiles + ROW_BUFFERS - 1) * TILE_M,), I32).at[pos].set(tok)
    tile_start = jnp.arange(n_tiles, dtype=I32) * TILE_M
    tile_expert = jnp.minimum(jnp.sum((tile_start[:, None] >= ends[None, :]).astype(I32), axis=1), N_EXPERTS - 1)
    pos = jnp.concatenate([pos.astype(I32), jnp.zeros(((ROW_BUFFERS - 1) * TILE_M,), I32)])
    return src_tok, pos, tile_expert


def _start_row_gather(idx_ref, base, n_rows, copy_row):
    for r in range(n_rows):
        copy_row(r, idx_ref[base + r], r % 2)


def _wait_rows(like_ref, dst_ref, sem):
    pltpu.make_async_copy(like_ref.at[pl.ds(0, dst_ref.shape[0])], dst_ref, sem).wait()


def _double_buffered_step(i, n_steps, bufs, sem, src_ref, fetch, compute):
    n = len(bufs)
    ahead = n - 1

    @pl.when(i == 0)
    def _():
        for t in range(ahead):
            fetch(t, bufs[t], sem.at[t])

    for slot in range(n):
        @pl.when(i % n == slot)
        def _(slot=slot):
            _wait_rows(src_ref, bufs[slot], sem.at[slot])
            nxt = (slot + ahead) % n
            fetch(i + ahead, bufs[nxt], sem.at[nxt])
            compute(bufs[slot])

    @pl.when(i == n_steps - 1)
    def _():
        for t in range(n_steps, n_steps + ahead):
            _wait_rows(src_ref, bufs[t % n], sem.at[t % n])


def _expert_kernel(te_ref, tok_ref, hp_ref, hp_like_ref, w1_ref, w3_ref, w2_ref, o_ref, *scratch, n_tiles):
    rows_refs, (w1b_ref, w3b_ref, w2b_ref, sem) = scratch[:ROW_BUFFERS], scratch[ROW_BUFFERS:]
    i = pl.program_id(0)

    @pl.when(jnp.logical_or(i == 0, te_ref[i] != te_ref[jnp.maximum(i - 1, 0)]))
    def _():
        w1b_ref[...] = w1_ref[0, 0].astype(BF16)
        w3b_ref[...] = w3_ref[0, 0].astype(BF16)
        w2b_ref[...] = w2_ref[0, 0].astype(BF16)

    def fetch(tile, dst_ref, dst_sem):
        def copy_row(r, tok, queue):
            pltpu.async_copy(hp_ref.at[tok], dst_ref.at[r // SUBLANES, :, r % SUBLANES, :], dst_sem, priority=queue)

        _start_row_gather(tok_ref, tile * TILE_M, TILE_M, copy_row)

    def compute(rows_ref):
        half = w1b_ref.shape[0] // 2
        words = jnp.concatenate([rows_ref[:, c, :, :].reshape(TILE_M, LANES) for c in range(SUBLANES)], axis=1)
        lo, hi = _unpack_words(words)
        lo, hi = lo.astype(BF16), hi.astype(BF16)
        a = (jnp.dot(lo, w1b_ref[:half, :], preferred_element_type=F32)
             + jnp.dot(hi, w1b_ref[half:, :], preferred_element_type=F32))
        b = (jnp.dot(lo, w3b_ref[:half, :], preferred_element_type=F32)
             + jnp.dot(hi, w3b_ref[half:, :], preferred_element_type=F32))
        hid = (a * _sigmoid_t(a) * b).astype(BF16)
        o_ref[...] = _pack_words(jnp.dot(hid, w2b_ref[...], preferred_element_type=F32))

    _double_buffered_step(i, n_tiles, rows_refs, sem, hp_like_ref, fetch, compute)


def _experts(hp, src_tok, tile_expert, w1, w3, w2, layer):
    n_tiles = tile_expert.shape[0]
    _, _, d, de = w1.shape
    t_all = hp.shape[0] // SUBLANES
    groups = TILE_M // SUBLANES
    rows = pltpu.VMEM((groups, d // 2 // LANES, SUBLANES, LANES), U32)
    return pl.pallas_call(
        functools.partial(_expert_kernel, n_tiles=n_tiles),
        grid_spec=pltpu.PrefetchScalarGridSpec(
            num_scalar_prefetch=2,
            grid=(n_tiles,),
            in_specs=[pl.BlockSpec(memory_space=pl.ANY), pl.BlockSpec(memory_space=pl.ANY),
                      pl.BlockSpec((1, 1, d, de), lambda i, te, tok: (layer, te[i], 0, 0)),
                      pl.BlockSpec((1, 1, d, de), lambda i, te, tok: (layer, te[i], 0, 0)),
                      pl.BlockSpec((1, 1, de, d), lambda i, te, tok: (layer, te[i], 0, 0))],
            out_specs=pl.BlockSpec((TILE_M, d // 2), lambda i, te, tok: (i, 0)),
            scratch_shapes=[rows] * ROW_BUFFERS + [pltpu.VMEM((d, de), BF16), pltpu.VMEM((d, de), BF16),
                                                   pltpu.VMEM((de, d), BF16), pltpu.SemaphoreType.DMA((ROW_BUFFERS,))]),
        out_shape=jax.ShapeDtypeStruct((n_tiles * TILE_M, d // 2), U32),
        compiler_params=_params("arbitrary"),
        name="moe_experts",
    )(tile_expert, src_tok, hp.reshape(t_all, SUBLANES, LANES),
      hp.reshape(t_all // SUBLANES, SUBLANES, SUBLANES, LANES), w1, w3, w2)


def _combine_kernel(pos_ref, ys_ref, wt_ref, x_ref, g2_ref, lg_ref, lb_ref, o_ref, *scratch, alpha, t_all, n_tiles):
    rows_refs, sem = scratch[:ROW_BUFFERS], scratch[ROW_BUFFERS]
    def fetch(tile, dst_ref, dst_sem):
        for k in range(2):
            def copy_row(r, row, queue, k=k):
                pltpu.async_copy(ys_ref.at[pl.ds(row, 1)], dst_ref.at[pl.ds(k * TILE_M + r, 1)], dst_sem, priority=queue)

            _start_row_gather(pos_ref, k * t_all + tile * TILE_M, TILE_M, copy_row)

    def compute(rows_ref):
        w = wt_ref[...]
        lo0, hi0 = _unpack_words(rows_ref[:TILE_M, :])
        lo1, hi1 = _unpack_words(rows_ref[TILE_M:, :])
        w0, w1 = w[:, 2:3], w[:, 3:4]
        y = jnp.concatenate([w0 * lo0 + w1 * lo1, w0 * hi0 + w1 * hi1], axis=1)
        o_ref[0] = _layernorm(alpha * x_ref[0] + g2_ref[0] * y, lg_ref[...], lb_ref[...])

    _double_buffered_step(pl.program_id(0), n_tiles, rows_refs, sem, ys_ref, fetch, compute)


def _combine(ys, pos, wt, x, g2, ln_g, ln_b, n_ctx_tiles, alpha, latent_only):
    bsz, l_all, d = x.shape
    nt = l_all // TILE_M
    rows = pltpu.VMEM((2 * TILE_M, d // 2), U32)
    if latent_only:
        out_rows = l_all - n_ctx_tiles * TILE_M
        out_map = lambda i, pos: (i // nt, jnp.maximum(i % nt - n_ctx_tiles, 0), 0)
    else:
        out_rows = l_all
        out_map = lambda i, pos: (i // nt, i % nt, 0)
    return pl.pallas_call(
        functools.partial(_combine_kernel, alpha=alpha, t_all=bsz * l_all, n_tiles=bsz * nt),
        grid_spec=pltpu.PrefetchScalarGridSpec(
            num_scalar_prefetch=1,
            grid=(bsz * nt,),
            in_specs=[pl.BlockSpec(memory_space=pl.ANY),
                      pl.BlockSpec((TILE_M, LANES), lambda i, pos: (i, 0)),
                      pl.BlockSpec((1, TILE_M, d), lambda i, pos: (i // nt, i % nt, 0)),
                      pl.BlockSpec((1, 1, d), lambda i, pos: (2 * (i // nt) + (i % nt >= n_ctx_tiles).astype(I32), 0, 0)),
                      pl.BlockSpec(ln_g.shape, lambda i, pos: (0, 0), pipeline_mode=pl.Buffered(1)),
                      pl.BlockSpec(ln_b.shape, lambda i, pos: (0, 0), pipeline_mode=pl.Buffered(1))],
            out_specs=pl.BlockSpec((1, TILE_M, d), out_map),
            scratch_shapes=[rows] * ROW_BUFFERS + [pltpu.SemaphoreType.DMA((ROW_BUFFERS,))]),
        out_shape=jax.ShapeDtypeStruct((bsz, out_rows, d), F32),
        compiler_params=_params("arbitrary"),
        name="moe_combine",
    )(pos, ys, wt, x, g2, ln_g, ln_b)


def _rope_tables(n_ctx, n_lat):
    rows = n_lat // GRID_W
    row = jnp.repeat(jnp.arange(rows, dtype=F32), GRID_W)
    col = jnp.tile(jnp.arange(GRID_W, dtype=F32), rows)
    n_freq = ROPE_DIM // 4
    inv = ROPE_BASE ** (-jnp.arange(n_freq, dtype=F32) / n_freq)
    ang = jnp.concatenate([row[:, None] * inv, col[:, None] * inv], axis=-1)
    cos = jnp.concatenate([jnp.ones((n_ctx, ROPE_DIM // 2), F32), jnp.cos(ang)], axis=0)
    sin = jnp.concatenate([jnp.zeros((n_ctx, ROPE_DIM // 2), F32), jnp.sin(ang)], axis=0)
    reps = LANES // ROPE_DIM
    return jnp.tile(jnp.concatenate([cos, cos], axis=1), (1, reps)), jnp.tile(jnp.concatenate([-sin, sin], axis=1), (1, reps))


def _split_w_in(w):
    d = w.shape[0]
    o = 0
    segs = {}
    for name, width in (("ckv", MLA_KV_RANK), ("kpe", MLA_ROPE), ("dk", 512), ("dv", 512), ("cq", MLA_Q_RANK),
                        ("dq", 512), ("gm", 2 * GMLP_W), ("pl", 512), ("gate", N_BRANCH * d)):
        segs[name] = w[:, o:o + width]
        o += width
    small = jnp.concatenate([segs["dk"], segs["dv"], segs["dq"], segs["pl"], segs["gm"], segs["ckv"], segs["kpe"],
                             jnp.zeros((d, LANES - MLA_ROPE), w.dtype), segs["cq"]], axis=1)
    return small.astype(BF16), segs["gate"].astype(BF16)


def _forward(x, c, ctx, c_ctx, ada_w, ada_b, w_in, mla_q_norm, mla_kv_norm, mla_w_uq, mla_w_ukv, diff_lambda,
             diff_subln, gmlp_ln_g, gmlp_ln_b, gmlp_ws, gmlp_bs, pool_w, pool_scale, w_branch, w_out, ln1_g, ln1_b,
             ln2_g, ln2_b, router_w, router_bias, moe_w1, moe_w3, moe_w2):
    bsz, n_lat, d = x.shape
    n_ctx = ctx.shape[1]
    depth = w_in.shape[0]
    l_all = n_ctx + n_lat
    t_all = bsz * l_all
    n_ctx_tiles = n_ctx // TILE_M
    alpha = (2 * depth) ** 0.25
    n_row_tiles = (2 * t_all) // TILE_M + N_EXPERTS

    xs = jnp.concatenate([ctx, x], axis=1)
    cos_t, sin_t = _rope_tables(n_ctx, n_lat)

    c_rows = ((bsz + 1 + SUBLANES - 1) // SUBLANES) * SUBLANES
    c_all = jnp.zeros((c_rows, d), F32).at[:bsz].set(c).at[bsz].set(c_ctx)
    mod = _ada(c_all, ada_w, ada_b)
    pick = jnp.stack([jnp.full((bsz,), bsz, I32), jnp.arange(bsz, dtype=I32)], axis=1).reshape(-1)
    mod = mod[:, pick].reshape(depth, 2 * bsz, 1, 6, d)

    rw = router_w.T
    rw_hi = rw.astype(BF16)
    rw_lo = (rw - rw_hi.astype(F32)).astype(BF16)
    rb = jnp.broadcast_to(router_bias.astype(F32)[:, None], (N_EXPERTS, TILE_M))
    w_in16 = w_in.astype(BF16)

    for l in range(depth):
        sh1, sc1, g1, sh2, sc2, g2 = (mod[l, :, :, k] for k in range(6))
        lam_init = 0.8 - 0.6 * math.exp(-0.3 * l)
        w_small, w_gate = _split_w_in(w_in16[l])
        w_ukv = mla_w_ukv[l].reshape(MLA_KV_RANK, MLA_HEADS, 2, MLA_NOPE).transpose(0, 2, 1, 3).reshape(MLA_KV_RANK, -1)
        w_uq = jnp.pad(mla_w_uq[l].reshape(MLA_Q_RANK, MLA_HEADS, MLA_NOPE + MLA_ROPE),
                       ((0, 0), (0, 0), (0, MLA_HEAD_PAD - MLA_NOPE - MLA_ROPE))).reshape(MLA_Q_RANK, -1)

        p = _inproj(xs, sc1, sh1, w_small, N_SMALL, n_ctx_tiles, "inproj_small")
        gl = _inproj(xs, sc1, sh1, w_gate, 4096, n_ctx_tiles, "inproj_gate")
        mq, mk, mv, dqr, dkr, dvx = _prep(p, cos_t, sin_t, mla_kv_norm[l][None], mla_q_norm[l][None],
                                          w_ukv.astype(BF16), w_uq.astype(BF16))
        mla_out = _mla_attention(mq, mk, mv, n_ctx)
        d_out = _diff_attention(dqr, dkr, dvx, diff_lambda[l], diff_subln[l][None], n_ctx, lam_init)
        g_out = _gmlp(p, gmlp_ln_g[l][None], gmlp_ln_b[l][None], gmlp_ws[l].astype(BF16), gmlp_bs[l].T)
        p_out = _pool(p, pool_w[l].astype(BF16), pool_scale[l][None], n_ctx)
        xs = _merge((mla_out, d_out, g_out, p_out), gl, xs, g1, w_branch[l].astype(BF16), w_out[l].astype(BF16),
                    ln1_g[l][None], ln1_b[l][None], n_ctx_tiles, alpha)

        hp, wt = _route(xs, sc2, sh2, rw_hi, rw_lo, rb, n_ctx_tiles)
        e_pairs = wt[:, :2].astype(I32).T.reshape(-1)
        src_tok, pos, tile_expert = _route_tables(e_pairs, t_all, n_row_tiles)
        ys = _experts(hp, src_tok, tile_expert, moe_w1, moe_w3, moe_w2, l)
        xs = _combine(ys, pos, wt, xs, g2, ln2_g[l][None], ln2_b[l][None], n_ctx_tiles, alpha, l == depth - 1)
    return xs


_forward_jit = jax.jit(_forward)


def kernel(x, c, ctx, c_ctx, ada_w, ada_b, w_in, mla_q_norm, mla_kv_norm, mla_w_uq, mla_w_ukv, diff_lambda, diff_subln, gmlp_ln_g, gmlp_ln_b, gmlp_ws, gmlp_bs, pool_w, pool_scale, w_branch, w_out, ln1_g, ln1_b, ln2_g, ln2_b, router_w, router_bias, moe_w1, moe_w3, moe_w2):
    return _forward_jit(x, c, ctx, c_ctx, ada_w, ada_b, w_in, mla_q_norm, mla_kv_norm, mla_w_uq, mla_w_ukv,
                        diff_lambda, diff_subln, gmlp_ln_g, gmlp_ln_b, gmlp_ws, gmlp_bs, pool_w, pool_scale,
                        w_branch, w_out, ln1_g, ln1_b, ln2_g, ln2_b, router_w, router_bias, moe_w1, moe_w3, moe_w2)
```

```python
import functools
import math

import jax
import jax.numpy as jnp
from jax import lax
from jax.experimental import pallas as pl
from jax.experimental.pallas import tpu as pltpu

F32 = jnp.float32
BF16 = jnp.bfloat16
U32 = jnp.uint32
I32 = jnp.int32

LANES = 128
SUBLANES = 8
TILE_M = 256
VMEM_LIMIT = 56 << 20

GRID_W = 64
ROPE_DIM = 64
ROPE_BASE = 10000.0
MLA_HEADS = 4
MLA_NOPE = 128
MLA_ROPE = 64
MLA_V = 128
MLA_Q_RANK = 384
MLA_KV_RANK = 256
MLA_HEAD_PAD = 256
DIFF_HEADS = 4
DIFF_QK = 64
DIFF_V = 128
CHUNK = 128
GMLP_GROUPS = 4
GMLP_W = 512
POOL_WINDOWS = (2, 4, 8, 16)
POOL_PAD = 16
N_BRANCH = 4
BRANCH_W = 512
N_EXPERTS = 32
N_GROUPS = 4
EXPERTS_PER_GROUP = 8
D_EXPERT = 512
LN_EPS = 1e-5
RMS_EPS = 1e-6

SEG_DK, SEG_DV, SEG_DQ, SEG_PL, SEG_GM, SEG_CKV, SEG_KPE, SEG_CQ = 0, 512, 1024, 1536, 2048, 3072, 3328, 3456
N_SMALL = 3840


def _params(*sem):
    return pltpu.CompilerParams(dimension_semantics=sem, vmem_limit_bytes=VMEM_LIMIT)


def _const_spec(shape):
    nd = len(shape)
    return pl.BlockSpec(shape, lambda *_: (0,) * nd, pipeline_mode=pl.Buffered(1))


def _sigmoid(x):
    return 1.0 / (1.0 + jnp.exp(-x))


def _sigmoid_t(x):
    return 0.5 * jnp.tanh(0.5 * x) + 0.5


def _layernorm(x, g, b):
    mu = jnp.mean(x, axis=-1, keepdims=True)
    xc = x - mu
    var = jnp.mean(xc * xc, axis=-1, keepdims=True)
    return xc * lax.rsqrt(var + LN_EPS) * g + b


def _rmsnorm(x, g):
    return x * lax.rsqrt(jnp.mean(x * x, axis=-1, keepdims=True) + RMS_EPS) * g


def _rope(x, cos_t, sin_t):
    lane = lax.broadcasted_iota(I32, (x.shape[0], LANES), 1)
    first_half = (lane % ROPE_DIM) < (ROPE_DIM // 2)
    outs = []
    for j in range(x.shape[1] // LANES):
        xc = x[:, j * LANES:(j + 1) * LANES]
        partner = jnp.where(first_half, pltpu.roll(xc, LANES - ROPE_DIM // 2, 1), pltpu.roll(xc, ROPE_DIM // 2, 1))
        outs.append(xc * cos_t + partner * sin_t)
    return outs[0] if len(outs) == 1 else jnp.concatenate(outs, axis=1)


def _ada_kernel(c_ref, w_ref, b_ref, o_ref):
    c = c_ref[...]
    s = (c * _sigmoid(c)).astype(BF16)
    o_ref[0] = jnp.dot(s, w_ref[0].astype(BF16), preferred_element_type=F32) + b_ref[0]


def _ada(c_all, ada_w, ada_b):
    depth, d, n6 = ada_w.shape
    tn = 1024
    rows = c_all.shape[0]
    return pl.pallas_call(
        _ada_kernel,
        grid=(depth, n6 // tn),
        in_specs=[pl.BlockSpec((rows, d), lambda l, j: (0, 0)),
                  pl.BlockSpec((1, d, tn), lambda l, j: (l, 0, j)),
                  pl.BlockSpec((1, 1, tn), lambda l, j: (l, 0, j))],
        out_specs=pl.BlockSpec((1, rows, tn), lambda l, j: (l, 0, j)),
        out_shape=jax.ShapeDtypeStruct((depth, rows, n6), F32),
        compiler_params=_params("parallel", "parallel"),
        name="ada",
    )(c_all, ada_w, ada_b.reshape(depth, 1, n6))


def _inproj_kernel(x_ref, sc_ref, sh_ref, w_ref, o_ref):
    h = (x_ref[0] * (1.0 + sc_ref[0]) + sh_ref[0]).astype(BF16)
    o_ref[0] = jnp.dot(h, w_ref[...], preferred_element_type=F32).astype(o_ref.dtype)


def _mod_spec(n_ctx_tiles, d, order):
    if order == "jbq":
        return pl.BlockSpec((1, 1, d), lambda j, b, q: (2 * b + (q >= n_ctx_tiles).astype(I32), 0, 0))
    return pl.BlockSpec((1, 1, d), lambda b, q: (2 * b + (q >= n_ctx_tiles).astype(I32), 0, 0))


def _inproj(x, sc, sh, w, tn, n_ctx_tiles, name):
    bsz, l_all, d = x.shape
    n = w.shape[1]
    nt = l_all // TILE_M
    return pl.pallas_call(
        _inproj_kernel,
        grid=(n // tn, bsz, nt),
        in_specs=[pl.BlockSpec((1, TILE_M, d), lambda j, b, q: (b, q, 0)),
                  _mod_spec(n_ctx_tiles, d, "jbq"), _mod_spec(n_ctx_tiles, d, "jbq"),
                  pl.BlockSpec((d, tn), lambda j, b, q: (0, j))],
        out_specs=pl.BlockSpec((1, TILE_M, tn), lambda j, b, q: (b, q, j)),
        out_shape=jax.ShapeDtypeStruct((bsz, l_all, n), BF16),
        compiler_params=_params("parallel", "parallel", "parallel"),
        name=name,
    )(x, sc, sh, w)


def _prep_kernel(ckv_ref, kpe_ref, cq_ref, dq_ref, dk_ref, dv_ref, cos_ref, sin_ref, gkv_ref, gq_ref, wukv_ref, wuq_ref,
                 mq_ref, mk_ref, mv_ref, dqr_ref, dkr_ref, dvx_ref, *, mla_scale, diff_scale):
    cos_t = cos_ref[...]
    sin_t = sin_ref[...]
    ckv = _rmsnorm(ckv_ref[0].astype(F32), gkv_ref[...]).astype(BF16)
    kv = jnp.dot(ckv, wukv_ref[...], preferred_element_type=F32)
    kpe = _rope(kpe_ref[0].astype(F32), cos_t, sin_t).astype(BF16)
    lane = lax.broadcasted_iota(I32, (kv.shape[0], LANES), 1)
    ones_col = jnp.where(lane == 0, 1.0, 0.0).astype(BF16)
    for h in range(MLA_HEADS):
        base = h * MLA_HEAD_PAD
        mk_ref[0, :, base:base + MLA_NOPE] = kv[:, h * MLA_NOPE:(h + 1) * MLA_NOPE].astype(BF16)
        mk_ref[0, :, base + MLA_NOPE:base + MLA_HEAD_PAD] = kpe
        vcol = MLA_HEADS * MLA_NOPE + h * MLA_V
        mv_ref[0, :, 2 * h * LANES:(2 * h + 1) * LANES] = kv[:, vcol:vcol + MLA_V].astype(BF16)
        mv_ref[0, :, (2 * h + 1) * LANES:(2 * h + 2) * LANES] = ones_col
        dvx_ref[0, :, 2 * h * LANES:(2 * h + 1) * LANES] = dv_ref[0, :, h * DIFF_V:(h + 1) * DIFF_V]
        dvx_ref[0, :, (2 * h + 1) * LANES:(2 * h + 2) * LANES] = ones_col
    cq = _rmsnorm(cq_ref[0].astype(F32), gq_ref[...]).astype(BF16)
    q = jnp.dot(cq, wuq_ref[...], preferred_element_type=F32)
    for h in range(MLA_HEADS):
        base = h * MLA_HEAD_PAD
        mq_ref[0, :, base:base + MLA_NOPE] = (q[:, base:base + MLA_NOPE] * mla_scale).astype(BF16)
        qpe = _rope(q[:, base + MLA_NOPE:base + MLA_HEAD_PAD], cos_t, sin_t)
        mq_ref[0, :, base + MLA_NOPE:base + MLA_HEAD_PAD] = (qpe * mla_scale).astype(BF16)
    dqr_ref[0] = (_rope(dq_ref[0].astype(F32), cos_t, sin_t) * diff_scale).astype(BF16)
    dkr_ref[0] = _rope(dk_ref[0].astype(F32), cos_t, sin_t).astype(BF16)


def _prep(p, cos_t, sin_t, g_kv, g_q, w_ukv, w_uq):
    bsz, l_all, _ = p.shape
    nt = l_all // TILE_M

    def seg(width, offset):
        return pl.BlockSpec((1, TILE_M, width), lambda b, q: (b, q, offset // width))

    def out(width):
        return pl.BlockSpec((1, TILE_M, width), lambda b, q: (b, q, 0))

    log2e = math.log2(math.e)
    kern = functools.partial(_prep_kernel, mla_scale=(MLA_NOPE + MLA_ROPE) ** -0.5 * log2e,
                             diff_scale=DIFF_QK ** -0.5 * log2e)
    vx = MLA_HEADS * 2 * LANES
    widths = (MLA_HEADS * MLA_HEAD_PAD, MLA_HEADS * MLA_HEAD_PAD, vx, 512, 512, vx)
    return pl.pallas_call(
        kern,
        grid=(bsz, nt),
        in_specs=[seg(MLA_KV_RANK, SEG_CKV), seg(LANES, SEG_KPE), seg(MLA_Q_RANK, SEG_CQ), seg(512, SEG_DQ),
                  seg(512, SEG_DK), seg(512, SEG_DV),
                  pl.BlockSpec((TILE_M, LANES), lambda b, q: (q, 0)), pl.BlockSpec((TILE_M, LANES), lambda b, q: (q, 0)),
                  _const_spec(g_kv.shape), _const_spec(g_q.shape), _const_spec(w_ukv.shape), _const_spec(w_uq.shape)],
        out_specs=[out(w) for w in widths],
        out_shape=[jax.ShapeDtypeStruct((bsz, l_all, w), BF16) for w in widths],
        compiler_params=_params("parallel", "parallel"),
        name="attn_prep",
    )(p, p, p, p, p, p, cos_t, sin_t, g_kv, g_q, w_ukv, w_uq)


KEY_CHUNK = 2048
MLA_HEADS_PER_STEP = 2
DIFF_HEADS_PER_STEP = 1
ROW_BUFFERS = 3


def _key_chunks(n_ctx, n_all):
    step = math.gcd(n_all - n_ctx, KEY_CHUNK)
    return [(0, n_ctx)] + [(lo, step) for lo in range(n_ctx, n_all, step)]


def _softmax_pv(streams, k_ref, v_ref, chunks):
    def scores(stream, chunk):
        q, kc, _ = stream
        lo, n = chunk
        return lax.dot_general(q, k_ref[0, lo:lo + n, kc], (((1,), (1,)), ((), ())), preferred_element_type=F32)

    m = [None] * len(streams)
    acc = [None] * len(streams)
    s_next = [scores(st, chunks[0]) for st in streams]
    for c, (lo, n) in enumerate(chunks):
        for i, st in enumerate(streams):
            s = s_next[i]
            if c + 1 < len(chunks):
                s_next[i] = scores(st, chunks[c + 1])
            m_new = jnp.max(s, axis=-1, keepdims=True)
            if m[i] is not None:
                m_new = jnp.maximum(m[i], m_new)
            pv = jnp.dot(jnp.exp2(s - m_new).astype(BF16), v_ref[0, lo:lo + n, st[2]], preferred_element_type=F32)
            acc[i] = pv if acc[i] is None else acc[i] * jnp.exp2(m[i] - m_new) + pv
            m[i] = m_new
    return acc


def _normalised(acc, width):
    return acc[:, :width] / acc[:, width:width + 1]


def _mla_kernel(q_ref, k_ref, v_ref, o_ref, *, n_ctx):
    heads = q_ref.shape[2] // MLA_HEAD_PAD
    cols = [slice(h * MLA_HEAD_PAD, (h + 1) * MLA_HEAD_PAD) for h in range(heads)]
    streams = [(q_ref[0, :, c], c, c) for c in cols]
    chunks = _key_chunks(n_ctx, k_ref.shape[1])

    def run(chunks):
        for h, acc in enumerate(_softmax_pv(streams, k_ref, v_ref, chunks)):
            o_ref[0, :, h * MLA_V:(h + 1) * MLA_V] = _normalised(acc, MLA_V).astype(o_ref.dtype)

    is_ctx = pl.program_id(2) < n_ctx // TILE_M
    pl.when(is_ctx)(lambda: run([(0, n_ctx)]))
    pl.when(jnp.logical_not(is_ctx))(lambda: run(chunks))


def _mla_attention(mq, mk, mv, n_ctx):
    bsz, l_all, _ = mq.shape
    nt = l_all // TILE_M
    hw = MLA_HEADS_PER_STEP * MLA_HEAD_PAD
    return pl.pallas_call(
        functools.partial(_mla_kernel, n_ctx=n_ctx),
        grid=(bsz, MLA_HEADS // MLA_HEADS_PER_STEP, nt),
        in_specs=[pl.BlockSpec((1, TILE_M, hw), lambda b, h, q: (b, q, h)),
                  pl.BlockSpec((1, l_all, hw), lambda b, h, q: (b, 0, h)),
                  pl.BlockSpec((1, l_all, hw), lambda b, h, q: (b, 0, h))],
        out_specs=pl.BlockSpec((1, TILE_M, MLA_HEADS_PER_STEP * MLA_V), lambda b, h, q: (b, q, h)),
        out_shape=jax.ShapeDtypeStruct((bsz, l_all, MLA_HEADS * MLA_V), BF16),
        compiler_params=_params("parallel", "parallel", "parallel"),
        name="mla_attn",
    )(mq, mk, mv)


def _diff_kernel(q_ref, k_ref, v_ref, lam_ref, g_ref, o_ref, *, n_ctx, lam_init):
    lv = lam_ref[...]
    lam = (jnp.exp(jnp.sum(lv[0:1] * lv[1:2], axis=-1, keepdims=True))
           - jnp.exp(jnp.sum(lv[2:3] * lv[3:4], axis=-1, keepdims=True)) + lam_init)
    tq = q_ref.shape[1]
    qw = 2 * DIFF_QK
    heads = q_ref.shape[2] // qw
    lane = lax.broadcasted_iota(I32, (tq, qw), 1)
    streams = []
    for h in range(heads):
        q = q_ref[0, :, h * qw:(h + 1) * qw]
        q12 = jnp.concatenate([jnp.where(lane < DIFF_QK, q, jnp.zeros_like(q)),
                               jnp.where(lane >= DIFF_QK, q, jnp.zeros_like(q))], axis=0)
        streams.append((q12, slice(h * qw, (h + 1) * qw), slice(2 * h * LANES, 2 * (h + 1) * LANES)))
    chunks = _key_chunks(n_ctx, k_ref.shape[1])

    def run(chunks):
        for h, acc in enumerate(_softmax_pv(streams, k_ref, v_ref, chunks)):
            o = _normalised(acc[:tq], DIFF_V) - lam * _normalised(acc[tq:], DIFF_V)
            o_ref[0, :, h * DIFF_V:(h + 1) * DIFF_V] = (_rmsnorm(o, g_ref[...]) * (1.0 - lam_init)).astype(o_ref.dtype)

    is_ctx = pl.program_id(2) < n_ctx // TILE_M
    pl.when(is_ctx)(lambda: run([(0, n_ctx)]))
    pl.when(jnp.logical_not(is_ctx))(lambda: run(chunks))


def _diff_attention(dqr, dkr, dvx, lam_vec, g_sub, n_ctx, lam_init):
    bsz, l_all, _ = dqr.shape
    nt = l_all // TILE_M
    hs = DIFF_HEADS_PER_STEP
    w = hs * 2 * DIFF_QK
    return pl.pallas_call(
        functools.partial(_diff_kernel, n_ctx=n_ctx, lam_init=lam_init),
        grid=(bsz, DIFF_HEADS // hs, nt),
        in_specs=[pl.BlockSpec((1, TILE_M, w), lambda b, h, q: (b, q, h)),
                  pl.BlockSpec((1, l_all, w), lambda b, h, q: (b, 0, h)),
                  pl.BlockSpec((1, l_all, hs * 2 * LANES), lambda b, h, q: (b, 0, h)),
                  _const_spec(lam_vec.shape), _const_spec(g_sub.shape)],
        out_specs=pl.BlockSpec((1, TILE_M, hs * DIFF_V), lambda b, h, q: (b, q, h)),
        out_shape=jax.ShapeDtypeStruct((bsz, l_all, DIFF_HEADS * DIFF_V), BF16),
        compiler_params=_params("parallel", "parallel", "parallel"),
        name="diff_attn",
    )(dqr, dkr, dvx, lam_vec, g_sub)


def _gelu_tanh(x):
    return 0.5 * x * (1.0 + jnp.tanh(math.sqrt(2.0 / math.pi) * (x + 0.044715 * (x * x * x))))


def _gmlp_kernel(z_ref, g_ref, b_ref, ws_ref, bst_ref, o_ref):
    z = _gelu_tanh(z_ref[0].astype(F32))
    u = z[:, :GMLP_W]
    v = _layernorm(z[:, GMLP_W:], g_ref[...], b_ref[...]).astype(BF16)
    gw = GMLP_W // GMLP_GROUPS
    for c in range(z.shape[0] // CHUNK):
        rows = slice(c * CHUNK, (c + 1) * CHUNK)
        for g in range(GMLP_GROUPS):
            cols = slice(g * gw, (g + 1) * gw)
            s = jnp.dot(ws_ref[g], v[rows, cols], preferred_element_type=F32) + bst_ref[:, g:g + 1]
            o_ref[0, rows, cols] = (u[rows, cols] * s).astype(o_ref.dtype)


def _gmlp(p, ln_g, ln_b, ws, bs_t):
    bsz, l_all, _ = p.shape
    nt = l_all // TILE_M
    return pl.pallas_call(
        _gmlp_kernel,
        grid=(bsz, nt),
        in_specs=[pl.BlockSpec((1, TILE_M, 2 * GMLP_W), lambda b, q: (b, q, SEG_GM // (2 * GMLP_W))),
                  _const_spec(ln_g.shape), _const_spec(ln_b.shape), _const_spec(ws.shape), _const_spec(bs_t.shape)],
        out_specs=pl.BlockSpec((1, TILE_M, GMLP_W), lambda b, q: (b, q, 0)),
        out_shape=jax.ShapeDtypeStruct((bsz, l_all, GMLP_W), BF16),
        compiler_params=_params("parallel", "parallel"),
        name="gmlp",
    )(p, ln_g, ln_b, ws, bs_t)


def _pool_kernel(p_ref, w_ref, sc_ref, o_ref, pad_ref, *, n_ctx):
    n_all = p_ref.shape[1]
    gw = LANES
    t = lax.broadcasted_iota(I32, (n_all, gw), 0)
    seq_lo = jnp.where(t < n_ctx, 0, n_ctx)
    seq_hi = jnp.where(t < n_ctx, n_ctx, n_all)
    pad_ref[0:POOL_PAD, :] = jnp.zeros((POOL_PAD, gw), F32)
    pad_ref[POOL_PAD + n_all:, :] = jnp.zeros((POOL_PAD, gw), F32)
    for i, win in enumerate(POOL_WINDOWS):
        cols = slice(i * gw, (i + 1) * gw)
        x = p_ref[0, :, cols].astype(F32)
        pad_ref[POOL_PAD:POOL_PAD + n_all, :] = x
        acc = jnp.zeros((n_all, gw), F32)
        cnt = jnp.zeros((n_all, gw), F32)
        for d in range(-(win // 2), win - win // 2):
            valid = jnp.logical_and(t + d >= seq_lo, t + d < seq_hi)
            acc = acc + jnp.where(valid, pad_ref[POOL_PAD + d:POOL_PAD + d + n_all, :], 0.0)
            cnt = cnt + valid.astype(F32)
        resid = (acc / cnt - x).astype(BF16)
        o = jnp.dot(resid, w_ref[i], preferred_element_type=F32) * sc_ref[:, cols]
        o_ref[0, :, cols] = o.astype(o_ref.dtype)


def _pool(p, pool_w, pool_scale, n_ctx):
    bsz, l_all, _ = p.shape
    width = len(POOL_WINDOWS) * LANES
    return pl.pallas_call(
        functools.partial(_pool_kernel, n_ctx=n_ctx),
        grid=(bsz,),
        in_specs=[pl.BlockSpec((1, l_all, width), lambda b: (b, 0, SEG_PL // width)),
                  _const_spec(pool_w.shape), _const_spec(pool_scale.shape)],
        out_specs=pl.BlockSpec((1, l_all, width), lambda b: (b, 0, 0)),
        out_shape=jax.ShapeDtypeStruct((bsz, l_all, width), BF16),
        scratch_shapes=[pltpu.VMEM((l_all + 2 * POOL_PAD, LANES), F32)],
        compiler_params=_params("parallel"),
        name="pool",
    )(p, pool_w, pool_scale)


def _merge_kernel(b0_ref, b1_ref, b2_ref, b3_ref, gl_ref, x_ref, g1_ref, wb_ref, wo_ref, lg_ref, lb_ref, o_ref, *,
                  alpha):
    d = x_ref.shape[2]
    merged = None
    for n, br in enumerate((b0_ref, b1_ref, b2_ref, b3_ref)):
        t = jnp.dot(br[0], wb_ref[n], preferred_element_type=F32)
        term = _sigmoid_t(gl_ref[0, :, n * d:(n + 1) * d].astype(F32)) * t
        merged = term if merged is None else merged + term
    y = jnp.dot(merged.astype(BF16), wo_ref[...], preferred_element_type=F32)
    o_ref[0] = _layernorm(alpha * x_ref[0] + g1_ref[0] * y, lg_ref[...], lb_ref[...])


def _merge(branches, gl, x, g1, w_branch, w_out, ln_g, ln_b, n_ctx_tiles, alpha):
    bsz, l_all, d = x.shape
    nt = l_all // TILE_M
    row = lambda width: pl.BlockSpec((1, TILE_M, width), lambda b, q: (b, q, 0))
    return pl.pallas_call(
        functools.partial(_merge_kernel, alpha=alpha),
        grid=(bsz, nt),
        in_specs=[row(BRANCH_W)] * N_BRANCH + [row(N_BRANCH * d), row(d), _mod_spec(n_ctx_tiles, d, "bq"),
                                               _const_spec(w_branch.shape), _const_spec(w_out.shape),
                                               _const_spec(ln_g.shape), _const_spec(ln_b.shape)],
        out_specs=row(d),
        out_shape=jax.ShapeDtypeStruct((bsz, l_all, d), F32),
        compiler_params=_params("parallel", "parallel"),
        name="merge",
    )(*branches, gl, x, g1, w_branch, w_out, ln_g, ln_b)


def _pack_words(y):
    d = y.shape[1]
    bits = pltpu.bitcast(y.astype(BF16).astype(F32), U32)
    return (bits[:, :d // 2] >> 16) | bits[:, d // 2:]


def _unpack_words(w):
    return pltpu.bitcast(w << 16, F32), pltpu.bitcast(w & jnp.uint32(0xFFFF0000), F32)


def _store_token_tiles(ref, words):
    m = words.shape[0]
    for c in range(SUBLANES):
        ref[pl.ds(c, m, stride=SUBLANES), :] = words[:, c * LANES:(c + 1) * LANES]


def _route_kernel(x_ref, sc_ref, sh_ref, rwh_ref, rwl_ref, rb_ref, hp_ref, meta_ref):
    h = x_ref[0] * (1.0 + sc_ref[0]) + sh_ref[0]
    h_hi = h.astype(BF16)
    h_lo = (h - h_hi.astype(F32)).astype(BF16)
    nt = (((1,), (1,)), ((), ()))
    logits = (lax.dot_general(rwh_ref[...], h_hi, nt, preferred_element_type=F32)
              + lax.dot_general(rwh_ref[...], h_lo, nt, preferred_element_type=F32)
              + lax.dot_general(rwl_ref[...], h_hi, nt, preferred_element_type=F32))
    scores = _sigmoid(logits)
    biased = scores + rb_ref[...]
    m = h.shape[0]
    neg = jnp.float32(-jnp.inf)
    far = jnp.float32(N_EXPERTS)
    best = None
    for g in range(N_GROUPS):
        a = biased[g * EXPERTS_PER_GROUP:(g + 1) * EXPERTS_PER_GROUP, :]
        ex = (lax.broadcasted_iota(I32, a.shape, 0) + g * EXPERTS_PER_GROUP).astype(F32)
        m1 = jnp.max(a, axis=0, keepdims=True)
        i1 = jnp.min(jnp.where(a == m1, ex, far), axis=0, keepdims=True)
        a2 = jnp.where(ex == i1, neg, a)
        m2 = jnp.max(a2, axis=0, keepdims=True)
        i2 = jnp.min(jnp.where(a2 == m2, ex, far), axis=0, keepdims=True)
        gs = m1 + m2
        if best is None:
            best = (gs, i1, i2)
        else:
            take = gs > best[0]
            best = (jnp.where(take, gs, best[0]), jnp.where(take, i1, best[1]), jnp.where(take, i2, best[2]))
    _, e0, e1 = best
    ex_all = lax.broadcasted_iota(I32, scores.shape, 0).astype(F32)
    w0 = jnp.sum(jnp.where(ex_all == e0, scores, 0.0), axis=0, keepdims=True)
    w1 = jnp.sum(jnp.where(ex_all == e1, scores, 0.0), axis=0, keepdims=True)
    tot = w0 + w1
    rows = jnp.concatenate([e0, e1, w0 / tot, w1 / tot, jnp.zeros((LANES - 4, m), F32)], axis=0)
    meta_ref[...] = rows.T
    _store_token_tiles(hp_ref, _pack_words(h))


def _route(x, sc, sh, rw_hi, rw_lo, rb, n_ctx_tiles):
    bsz, l_all, d = x.shape
    nt = l_all // TILE_M
    t_all = bsz * l_all
    flat = lambda b, q: (b * nt + q, 0)
    return pl.pallas_call(
        _route_kernel,
        grid=(bsz, nt),
        in_specs=[pl.BlockSpec((1, TILE_M, d), lambda b, q: (b, q, 0)),
                  _mod_spec(n_ctx_tiles, d, "bq"), _mod_spec(n_ctx_tiles, d, "bq"),
                  _const_spec(rw_hi.shape), _const_spec(rw_lo.shape), _const_spec(rb.shape)],
        out_specs=[pl.BlockSpec((TILE_M * SUBLANES, LANES), flat), pl.BlockSpec((TILE_M, LANES), flat)],
        out_shape=[jax.ShapeDtypeStruct((t_all * SUBLANES, LANES), U32), jax.ShapeDtypeStruct((t_all, LANES), F32)],
        compiler_params=_params("parallel", "parallel"),
        name="moe_route",
    )(x, sc, sh, rw_hi, rw_lo, rb)


def _route_tables(e_pairs, t_all, n_tiles):
    onehot = (e_pairs[:, None] == jnp.arange(N_EXPERTS, dtype=I32)[None, :]).astype(I32)
    csum = jnp.cumsum(onehot, axis=0)
    rank = jnp.sum((csum - onehot) * onehot, axis=1)
    counts = csum[-1]
    padded = ((counts + TILE_M - 1) // TILE_M) * TILE_M
    ends = jnp.cumsum(padded)
    pos = (ends - padded)[e_pairs] + rank
    tok = jnp.tile(jnp.arange(t_all, dtype=I32), 2)
    src_tok = jnp.zeros(((n_tiles + ROW_BUFFERS - 1) * TILE_M,), I32).at[pos].set(tok)
    tile_start = jnp.arange(n_tiles, dtype=I32) * TILE_M
    tile_expert = jnp.minimum(jnp.sum((tile_start[:, None] >= ends[None, :]).astype(I32), axis=1), N_EXPERTS - 1)
    tile_expert = jnp.concatenate([tile_expert, ends[-1:].astype(I32) // TILE_M])
    pos = jnp.concatenate([pos.astype(I32), jnp.zeros(((ROW_BUFFERS - 1) * TILE_M,), I32)])
    return src_tok, pos, tile_expert


def _start_row_gather(idx_ref, base, n_rows, copy_row):
    for r in range(n_rows):
        copy_row(r, idx_ref[base + r], r % 2)


def _wait_rows(like_ref, dst_ref, sem):
    pltpu.make_async_copy(like_ref.at[pl.ds(0, dst_ref.shape[0])], dst_ref, sem).wait()


def _double_buffered_step(i, n_active, bufs, sem, src_ref, fetch, compute):
    n = len(bufs)
    ahead = n - 1

    @pl.when(i == 0)
    def _():
        for t in range(ahead):
            fetch(t, bufs[t], sem.at[t])

    for slot in range(n):
        @pl.when(jnp.logical_and(i % n == slot, i < n_active))
        def _(slot=slot):
            _wait_rows(src_ref, bufs[slot], sem.at[slot])
            nxt = (slot + ahead) % n
            fetch(i + ahead, bufs[nxt], sem.at[nxt])
            compute(bufs[slot])

            @pl.when(i == n_active - 1)
            def _():
                for a in range(1, n):
                    _wait_rows(src_ref, bufs[(slot + a) % n], sem.at[(slot + a) % n])


def _expert_kernel(te_ref, tok_ref, hp_ref, hp_like_ref, w1_ref, w3_ref, w2_ref, o_ref, *scratch, n_tiles):
    rows_refs, (w1b_ref, w3b_ref, w2b_ref, sem) = scratch[:ROW_BUFFERS], scratch[ROW_BUFFERS:]
    i = pl.program_id(0)
    n_active = te_ref[n_tiles]

    @pl.when(i >= n_active)
    def _():
        o_ref[...] = jnp.zeros(o_ref.shape, o_ref.dtype)

    @pl.when(jnp.logical_and(i < n_active, jnp.logical_or(i == 0, te_ref[i] != te_ref[jnp.maximum(i - 1, 0)])))
    def _():
        w1b_ref[...] = w1_ref[0, 0].astype(BF16)
        w3b_ref[...] = w3_ref[0, 0].astype(BF16)
        w2b_ref[...] = w2_ref[0, 0].astype(BF16)

    def fetch(tile, dst_ref, dst_sem):
        def copy_row(r, tok, queue):
            pltpu.async_copy(hp_ref.at[tok], dst_ref.at[r // SUBLANES, :, r % SUBLANES, :], dst_sem, priority=queue)

        _start_row_gather(tok_ref, tile * TILE_M, TILE_M, copy_row)

    def compute(rows_ref):
        half = w1b_ref.shape[0] // 2
        words = jnp.concatenate([rows_ref[:, c, :, :].reshape(TILE_M, LANES) for c in range(SUBLANES)], axis=1)
        lo, hi = _unpack_words(words)
        lo, hi = lo.astype(BF16), hi.astype(BF16)
        a = (jnp.dot(lo, w1b_ref[:half, :], preferred_element_type=F32)
             + jnp.dot(hi, w1b_ref[half:, :], preferred_element_type=F32))
        b = (jnp.dot(lo, w3b_ref[:half, :], preferred_element_type=F32)
             + jnp.dot(hi, w3b_ref[half:, :], preferred_element_type=F32))
        hid = (a * _sigmoid_t(a) * b).astype(BF16)
        o_ref[...] = _pack_words(jnp.dot(hid, w2b_ref[...], preferred_element_type=F32))

    _double_buffered_step(i, n_active, rows_refs, sem, hp_like_ref, fetch, compute)


def _experts(hp, src_tok, tile_expert, w1, w3, w2, layer):
    n_tiles = tile_expert.shape[0] - 1
    _, _, d, de = w1.shape
    t_all = hp.shape[0] // SUBLANES
    groups = TILE_M // SUBLANES
    rows = pltpu.VMEM((groups, d // 2 // LANES, SUBLANES, LANES), U32)
    return pl.pallas_call(
        functools.partial(_expert_kernel, n_tiles=n_tiles),
        grid_spec=pltpu.PrefetchScalarGridSpec(
            num_scalar_prefetch=2,
            grid=(n_tiles,),
            in_specs=[pl.BlockSpec(memory_space=pl.ANY), pl.BlockSpec(memory_space=pl.ANY),
                      pl.BlockSpec((1, 1, d, de), lambda i, te, tok: (layer, te[i], 0, 0)),
                      pl.BlockSpec((1, 1, d, de), lambda i, te, tok: (layer, te[i], 0, 0)),
                      pl.BlockSpec((1, 1, de, d), lambda i, te, tok: (layer, te[i], 0, 0))],
            out_specs=pl.BlockSpec((TILE_M, d // 2), lambda i, te, tok: (i, 0)),
            scratch_shapes=[rows] * ROW_BUFFERS + [pltpu.VMEM((d, de), BF16), pltpu.VMEM((d, de), BF16),
                                                   pltpu.VMEM((de, d), BF16), pltpu.SemaphoreType.DMA((ROW_BUFFERS,))]),
        out_shape=jax.ShapeDtypeStruct((n_tiles * TILE_M, d // 2), U32),
        compiler_params=_params("arbitrary"),
        name="moe_experts",
    )(tile_expert, src_tok, hp.reshape(t_all, SUBLANES, LANES),
      hp.reshape(t_all // SUBLANES, SUBLANES, SUBLANES, LANES), w1, w3, w2)


def _combine_kernel(pos_ref, ys_ref, wt_ref, x_ref, g2_ref, lg_ref, lb_ref, o_ref, *scratch, alpha, t_all, n_tiles):
    rows_refs, sem = scratch[:ROW_BUFFERS], scratch[ROW_BUFFERS]
    def fetch(tile, dst_ref, dst_sem):
        for k in range(2):
            def copy_row(r, row, queue, k=k):
                pltpu.async_copy(ys_ref.at[pl.ds(row, 1)], dst_ref.at[pl.ds(k * TILE_M + r, 1)], dst_sem, priority=queue)

            _start_row_gather(pos_ref, k * t_all + tile * TILE_M, TILE_M, copy_row)

    def compute(rows_ref):
        w = wt_ref[...]
        lo0, hi0 = _unpack_words(rows_ref[:TILE_M, :])
        lo1, hi1 = _unpack_words(rows_ref[TILE_M:, :])
        w0, w1 = w[:, 2:3], w[:, 3:4]
        y = jnp.concatenate([w0 * lo0 + w1 * lo1, w0 * hi0 + w1 * hi1], axis=1)
        o_ref[0] = _layernorm(alpha * x_ref[0] + g2_ref[0] * y, lg_ref[...], lb_ref[...])

    _double_buffered_step(pl.program_id(0), n_tiles, rows_refs, sem, ys_ref, fetch, compute)


def _combine(ys, pos, wt, x, g2, ln_g, ln_b, n_ctx_tiles, alpha, latent_only):
    bsz, l_all, d = x.shape
    nt = l_all // TILE_M
    rows = pltpu.VMEM((2 * TILE_M, d // 2), U32)
    if latent_only:
        out_rows = l_all - n_ctx_tiles * TILE_M
        out_map = lambda i, pos: (i // nt, jnp.maximum(i % nt - n_ctx_tiles, 0), 0)
    else:
        out_rows = l_all
        out_map = lambda i, pos: (i // nt, i % nt, 0)
    return pl.pallas_call(
        functools.partial(_combine_kernel, alpha=alpha, t_all=bsz * l_all, n_tiles=bsz * nt),
        grid_spec=pltpu.PrefetchScalarGridSpec(
            num_scalar_prefetch=1,
            grid=(bsz * nt,),
            in_specs=[pl.BlockSpec(memory_space=pl.ANY),
                      pl.BlockSpec((TILE_M, LANES), lambda i, pos: (i, 0)),
                      pl.BlockSpec((1, TILE_M, d), lambda i, pos: (i // nt, i % nt, 0)),
                      pl.BlockSpec((1, 1, d), lambda i, pos: (2 * (i // nt) + (i % nt >= n_ctx_tiles).astype(I32), 0, 0)),
                      pl.BlockSpec(ln_g.shape, lambda i, pos: (0, 0), pipeline_mode=pl.Buffered(1)),
                      pl.BlockSpec(ln_b.shape, lambda i, pos: (0, 0), pipeline_mode=pl.Buffered(1))],
            out_specs=pl.BlockSpec((1, TILE_M, d), out_map),
            scratch_shapes=[rows] * ROW_BUFFERS + [pltpu.SemaphoreType.DMA((ROW_BUFFERS,))]),
        out_shape=jax.ShapeDtypeStruct((bsz, out_rows, d), F32),
        compiler_params=_params("arbitrary"),
        name="moe_combine",
    )(pos, ys, wt, x, g2, ln_g, ln_b)


def _rope_tables(n_ctx, n_lat):
    rows = n_lat // GRID_W
    row = jnp.repeat(jnp.arange(rows, dtype=F32), GRID_W)
    col = jnp.tile(jnp.arange(GRID_W, dtype=F32), rows)
    n_freq = ROPE_DIM // 4
    inv = ROPE_BASE ** (-jnp.arange(n_freq, dtype=F32) / n_freq)
    ang = jnp.concatenate([row[:, None] * inv, col[:, None] * inv], axis=-1)
    cos = jnp.concatenate([jnp.ones((n_ctx, ROPE_DIM // 2), F32), jnp.cos(ang)], axis=0)
    sin = jnp.concatenate([jnp.zeros((n_ctx, ROPE_DIM // 2), F32), jnp.sin(ang)], axis=0)
    reps = LANES // ROPE_DIM
    return jnp.tile(jnp.concatenate([cos, cos], axis=1), (1, reps)), jnp.tile(jnp.concatenate([-sin, sin], axis=1), (1, reps))


def _split_w_in(w):
    d = w.shape[0]
    o = 0
    segs = {}
    for name, width in (("ckv", MLA_KV_RANK), ("kpe", MLA_ROPE), ("dk", 512), ("dv", 512), ("cq", MLA_Q_RANK),
                        ("dq", 512), ("gm", 2 * GMLP_W), ("pl", 512), ("gate", N_BRANCH * d)):
        segs[name] = w[:, o:o + width]
        o += width
    small = jnp.concatenate([segs["dk"], segs["dv"], segs["dq"], segs["pl"], segs["gm"], segs["ckv"], segs["kpe"],
                             jnp.zeros((d, LANES - MLA_ROPE), w.dtype), segs["cq"]], axis=1)
    return small.astype(BF16), segs["gate"].astype(BF16)


def _forward(x, c, ctx, c_ctx, ada_w, ada_b, w_in, mla_q_norm, mla_kv_norm, mla_w_uq, mla_w_ukv, diff_lambda,
             diff_subln, gmlp_ln_g, gmlp_ln_b, gmlp_ws, gmlp_bs, pool_w, pool_scale, w_branch, w_out, ln1_g, ln1_b,
             ln2_g, ln2_b, router_w, router_bias, moe_w1, moe_w3, moe_w2):
    bsz, n_lat, d = x.shape
    n_ctx = ctx.shape[1]
    depth = w_in.shape[0]
    l_all = n_ctx + n_lat
    t_all = bsz * l_all
    n_ctx_tiles = n_ctx // TILE_M
    alpha = (2 * depth) ** 0.25
    n_row_tiles = (2 * t_all) // TILE_M + N_EXPERTS

    xs = jnp.concatenate([ctx, x], axis=1)
    cos_t, sin_t = _rope_tables(n_ctx, n_lat)

    c_rows = ((bsz + 1 + SUBLANES - 1) // SUBLANES) * SUBLANES
    c_all = jnp.zeros((c_rows, d), F32).at[:bsz].set(c).at[bsz].set(c_ctx)
    mod = _ada(c_all, ada_w, ada_b)
    pick = jnp.stack([jnp.full((bsz,), bsz, I32), jnp.arange(bsz, dtype=I32)], axis=1).reshape(-1)
    mod = mod[:, pick].reshape(depth, 2 * bsz, 1, 6, d)

    rw = router_w.T
    rw_hi = rw.astype(BF16)
    rw_lo = (rw - rw_hi.astype(F32)).astype(BF16)
    rb = jnp.broadcast_to(router_bias.astype(F32)[:, None], (N_EXPERTS, TILE_M))
    w_in16 = w_in.astype(BF16)

    for l in range(depth):
        sh1, sc1, g1, sh2, sc2, g2 = (mod[l, :, :, k] for k in range(6))
        lam_init = 0.8 - 0.6 * math.exp(-0.3 * l)
        w_small, w_gate = _split_w_in(w_in16[l])
        w_ukv = mla_w_ukv[l].reshape(MLA_KV_RANK, MLA_HEADS, 2, MLA_NOPE).transpose(0, 2, 1, 3).reshape(MLA_KV_RANK, -1)
        w_uq = jnp.pad(mla_w_uq[l].reshape(MLA_Q_RANK, MLA_HEADS, MLA_NOPE + MLA_ROPE),
                       ((0, 0), (0, 0), (0, MLA_HEAD_PAD - MLA_NOPE - MLA_ROPE))).reshape(MLA_Q_RANK, -1)

        p = _inproj(xs, sc1, sh1, w_small, N_SMALL, n_ctx_tiles, "inproj_small")
        gl = _inproj(xs, sc1, sh1, w_gate, 4096, n_ctx_tiles, "inproj_gate")
        mq, mk, mv, dqr, dkr, dvx = _prep(p, cos_t, sin_t, mla_kv_norm[l][None], mla_q_norm[l][None],
                                          w_ukv.astype(BF16), w_uq.astype(BF16))
        mla_out = _mla_attention(mq, mk, mv, n_ctx)
        d_out = _diff_attention(dqr, dkr, dvx, diff_lambda[l], diff_subln[l][None], n_ctx, lam_init)
        g_out = _gmlp(p, gmlp_ln_g[l][None], gmlp_ln_b[l][None], gmlp_ws[l].astype(BF16), gmlp_bs[l].T)
        p_out = _pool(p, pool_w[l].astype(BF16), pool_scale[l][None], n_ctx)
        xs = _merge((mla_out, d_out, g_out, p_out), gl, xs, g1, w_branch[l].astype(BF16), w_out[l].astype(BF16),
                    ln1_g[l][None], ln1_b[l][None], n_ctx_tiles, alpha)

        hp, wt = _route(xs, sc2, sh2, rw_hi, rw_lo, rb, n_ctx_tiles)
        e_pairs = wt[:, :2].astype(I32).T.reshape(-1)
        src_tok, pos, tile_expert = _route_tables(e_pairs, t_all, n_row_tiles)
        ys = _experts(hp, src_tok, tile_expert, moe_w1, moe_w3, moe_w2, l)
        xs = _combine(ys, pos, wt, xs, g2, ln2_g[l][None], ln2_b[l][None], n_ctx_tiles, alpha, l == depth - 1)
    return xs


_forward_jit = jax.jit(_forward)


def kernel(x, c, ctx, c_ctx, ada_w, ada_b, w_in, mla_q_norm, mla_kv_norm, mla_w_uq, mla_w_ukv, diff_lambda, diff_subln, gmlp_ln_g, gmlp_ln_b, gmlp_ws, gmlp_bs, pool_w, pool_scale, w_branch, w_out, ln1_g, ln1_b, ln2_g, ln2_b, router_w, router_bias, moe_w1, moe_w3, moe_w2):
    return _forward_jit(x, c, ctx, c_ctx, ada_w, ada_b, w_in, mla_q_norm, mla_kv_norm, mla_w_uq, mla_w_ukv,
                        diff_lambda, diff_subln, gmlp_ln_g, gmlp_ln_b, gmlp_ws, gmlp_bs, pool_w, pool_scale,
                        w_branch, w_out, ln1_g, ln1_b, ln2_g, ln2_b, router_w, router_bias, moe_w1, moe_w3, moe_w2)
```
